```python
import math
import jax, jax.numpy as jnp
from jax import lax
import numpy as np

D_MODEL = 1024
BATCH = 8
SEQ = 4096
DEPTH = 2

CHUNK = 64
Q_BLOCK = 128
HEAD_DIM = 64
N_HEADS_SPARSE = 8
N_HEADS_SB = 8
W_SPARSE = N_HEADS_SPARSE * HEAD_DIM
W_SB = N_HEADS_SB * HEAD_DIM
N_IDX_HEADS = 8
IDX_DIM = 64
INDEX_TOPK = 256
N_BUCKETS = 32
MAX_DISTANCE = 128
D_FF = -(-8 * D_MODEL // (3 * 256)) * 256
RMS_EPS = 1e-6
IN_SIZES = (W_SPARSE, W_SPARSE, W_SPARSE,
            N_IDX_HEADS * IDX_DIM, IDX_DIM, N_IDX_HEADS,
            W_SB, W_SB, W_SB,
            D_MODEL, D_MODEL)
IN_WIDTH = sum(IN_SIZES)

kernel_name = 'hybrid_dsa_stickbreaking_block'


def rmsnorm(x, g):
    xf = x.astype(jnp.float32)
    y = xf * lax.rsqrt(jnp.mean(xf * xf, axis=-1, keepdims=True) + RMS_EPS) * g.astype(jnp.float32)
    return y.astype(x.dtype)


def split_columns(p):
    offs = np.cumsum(np.array(IN_SIZES))[:-1].tolist()
    return jnp.split(p, offs, axis=-1)


def t5_bucket(rel):
    nb = N_BUCKETS // 2
    max_exact = nb // 2
    ret = (rel > 0).astype(jnp.int32) * nb
    n = jnp.abs(rel)
    nf = jnp.maximum(n, 1).astype(jnp.float32)
    large = max_exact + (jnp.log(nf / max_exact) / math.log(MAX_DISTANCE / max_exact)
                         * (nb - max_exact)).astype(jnp.int32)
    large = jnp.minimum(large, nb - 1)
    return ret + jnp.where(n < max_exact, n, large)


def sparse_attention(q, k, v, q_idx, k_idx, w_idx, rel_bias):
    B, S, H, dh = q.shape
    top_k = min(INDEX_TOPK, S // 4)
    spos = jnp.arange(S)
    k_idx_f = k_idx.astype(jnp.float32)

    def block(blk):
        t0 = blk * Q_BLOCK
        tpos = t0 + jnp.arange(Q_BLOCK)
        qi = lax.dynamic_slice_in_dim(q_idx, t0, Q_BLOCK, axis=1).astype(jnp.float32)
        wi = lax.dynamic_slice_in_dim(w_idx, t0, Q_BLOCK, axis=1).astype(jnp.float32)
        dots = jax.nn.relu(jnp.einsum('bqhd,bsd->bqhs', qi, k_idx_f) * IDX_DIM ** -0.5)
        score = jnp.einsum('bqhs,bqh->bqs', dots, wi * N_IDX_HEADS ** -0.5)
        admissible = (spos[None, :] // CHUNK) <= (tpos[:, None] // CHUNK)
        score = jnp.where(admissible[None], score, -jnp.inf)
        top_val, top_idx = lax.top_k(score, top_k)
        valid = jnp.isfinite(top_val)
        k_sel = jax.vmap(lambda kb, ib: kb[ib])(k, top_idx).astype(jnp.float32)
        v_sel = jax.vmap(lambda vb, ib: vb[ib])(v, top_idx).astype(jnp.float32)
        qa = lax.dynamic_slice_in_dim(q, t0, Q_BLOCK, axis=1).astype(jnp.float32)
        logits = jnp.einsum('bqhd,bqkhd->bhqk', qa, k_sel) * dh ** -0.5
        rel = top_idx - tpos[None, :, None]
        bias = rel_bias.astype(jnp.float32)[t5_bucket(rel)]
        logits = logits + bias.transpose(0, 3, 1, 2)
        logits = jnp.where(valid[:, None], logits, -jnp.inf)
        p = jax.nn.softmax(logits, axis=-1)
        return jnp.einsum('bhqk,bqkhd->bqhd', p, v_sel).astype(q.dtype)

    o = lax.map(block, jnp.arange(S // Q_BLOCK))
    return o.transpose(1, 0, 2, 3, 4).reshape(B, S, H * dh)


def stick_breaking_attention(q, k, v):
    B, S, H, dh = q.shape
    spos = jnp.arange(S)
    kf = k.astype(jnp.float32)
    vf = v.astype(jnp.float32)

    def block(blk):
        t0 = blk * Q_BLOCK
        tpos = t0 + jnp.arange(Q_BLOCK)
        qb = lax.dynamic_slice_in_dim(q, t0, Q_BLOCK, axis=1).astype(jnp.float32)
        z = jnp.einsum('bqhd,bshd->bhqs', qb, kf) * dh ** -0.5
        causal = (spos[None, :] < tpos[:, None])[None, None]
        log_beta = jax.nn.log_sigmoid(z)
        log_keep = jnp.where(causal, log_beta - z, 0.0)
        between = lax.cumsum(log_keep, axis=3, reverse=True) - log_keep
        a = jnp.where(causal, jnp.exp(log_beta + between), 0.0)
        return jnp.einsum('bhqs,bshd->bqhd', a, vf).astype(q.dtype)

    o = lax.map(block, jnp.arange(S // Q_BLOCK))
    return o.transpose(1, 0, 2, 3, 4).reshape(B, S, H * dh)


def setup_inputs(seed: int = 0) -> dict:
    key = jax.random.key(seed)
    ks = jax.random.split(key, 16)
    f32 = jnp.float32

    def w(k, shape, fan_in):
        return jax.random.normal(k, shape, f32) * fan_in ** -0.5

    def gain(k):
        return 1.0 + 0.05 * jax.random.normal(k, (DEPTH, D_MODEL), f32)

    return {
        'x': jax.random.normal(ks[0], (BATCH, SEQ, D_MODEL), f32),
        'w_in': w(ks[1], (DEPTH, D_MODEL, IN_WIDTH), D_MODEL),
        'w_branch_sparse': w(ks[2], (DEPTH, W_SPARSE, D_MODEL), W_SPARSE),
        'w_branch_sb': w(ks[3], (DEPTH, W_SB, D_MODEL), W_SB),
        'w_out': w(ks[4], (DEPTH, D_MODEL, D_MODEL), D_MODEL),
        'w_gate_up': w(ks[5], (DEPTH, D_MODEL, 2 * D_FF), D_MODEL),
        'w_down': w(ks[6], (DEPTH, D_FF, D_MODEL), D_FF),
        'g_pre_mix': gain(ks[7]),
        'g_post_mix': gain(ks[8]),
        'g_pre_ffn': gain(ks[9]),
        'g_post_ffn': gain(ks[10]),
        'rel_bias': 0.5 * jax.random.normal(ks[11], (N_BUCKETS, N_HEADS_SPARSE), f32),
    }


def reference(x, w_in, w_branch_sparse, w_branch_sb, w_out, w_gate_up, w_down,
              g_pre_mix, g_post_mix, g_pre_ffn, g_post_ffn, rel_bias):
    B, S, _ = x.shape
    for l in range(DEPTH):
        h = rmsnorm(x, g_pre_mix[l])
        proj = h @ w_in[l]
        (qa, ka, va, qi, ki, wi, qb, kb, vb, gate_a, gate_b) = split_columns(proj)
        o_a = sparse_attention(
            qa.reshape(B, S, N_HEADS_SPARSE, HEAD_DIM),
            ka.reshape(B, S, N_HEADS_SPARSE, HEAD_DIM),
            va.reshape(B, S, N_HEADS_SPARSE, HEAD_DIM),
            qi.reshape(B, S, N_IDX_HEADS, IDX_DIM), ki, wi, rel_bias)
        o_b = stick_breaking_attention(
            qb.reshape(B, S, N_HEADS_SB, HEAD_DIM),
            kb.reshape(B, S, N_HEADS_SB, HEAD_DIM),
            vb.reshape(B, S, N_HEADS_SB, HEAD_DIM))
        m = (jax.nn.sigmoid(gate_a) * (o_a @ w_branch_sparse[l])
             + jax.nn.sigmoid(gate_b) * (o_b @ w_branch_sb[l]))
        x = x + rmsnorm(m @ w_out[l], g_post_mix[l])
        h = rmsnorm(x, g_pre_ffn[l])
        g, u = jnp.split(h @ w_gate_up[l], 2, axis=-1)
        x = x + rmsnorm((jax.nn.silu(g) * u) @ w_down[l], g_post_ffn[l])
    return x
```

```python
import functools
import math

import jax
import jax.numpy as jnp
from jax import lax
from jax.experimental import pallas as pl
from jax.experimental.pallas import tpu as pltpu

F32 = jnp.float32
BF16 = jnp.bfloat16

CHUNK = 64
HEAD_DIM = 64
N_HEADS = 8
W_ATT = N_HEADS * HEAD_DIM
N_IDX_HEADS = 8
IDX_DIM = 64
INDEX_TOPK = 256
N_BUCKETS = 32
MAX_DISTANCE = 128
RMS_EPS = 1e-6

INT_MIN = -(2 ** 31)
MASK_NEG = -1e30
M_INIT = -1e29
SB_EXIT = -110.0

VMEM_LIMIT_BYTES = 58 * 1024 * 1024

NT_DIMS = (((1,), (1,)), ((), ()))


def _rms(x, g):
    ms = jnp.mean(x * x, axis=-1, keepdims=True)
    return x * lax.rsqrt(ms + RMS_EPS) * g


def _proj_kernel(x_ref, g_ref, w_ref, qkva_ref, qi_ref, qkvb_ref, gates_ref, kw_ref):
    h = _rms(x_ref[...], g_ref[...]).astype(BF16)

    def mm(c0, c1):
        return jnp.dot(h, w_ref[:, c0:c1], preferred_element_type=F32)

    W = W_ATT
    qkva_ref[:, 0:W] = (mm(0, W) * 0.125).astype(BF16)
    qkva_ref[:, W:2 * W] = mm(W, 2 * W).astype(BF16)
    qkva_ref[:, 2 * W:3 * W] = mm(2 * W, 3 * W).astype(BF16)
    qi_ref[...] = (mm(3 * W, 4 * W) * 0.125).astype(BF16)
    qkvb_ref[:, 0:W] = (mm(4 * W, 5 * W) * 0.125).astype(BF16)
    qkvb_ref[:, W:2 * W] = mm(5 * W, 6 * W).astype(BF16)
    qkvb_ref[:, 2 * W:3 * W] = mm(6 * W, 7 * W).astype(BF16)
    d = x_ref.shape[1]
    for c in range(0, 2 * d, W):
        gates_ref[:, c:c + W] = mm(7 * W + c, 7 * W + c + W)
    kw_ref[...] = mm(7 * W + 2 * d, 7 * W + 2 * d + 128)


def _proj(x2, g, w_packed, tm):
    n, d = x2.shape
    wcols = w_packed.shape[1]
    W = W_ATT
    return pl.pallas_call(
        _proj_kernel,
        grid=(n // tm,),
        in_specs=[
            pl.BlockSpec((tm, d), lambda i: (i, 0)),
            pl.BlockSpec((1, d), lambda i: (0, 0)),
            pl.BlockSpec((d, wcols), lambda i: (0, 0), pipeline_mode=pl.Buffered(1)),
        ],
        out_specs=[
            pl.BlockSpec((tm, 3 * W), lambda i: (i, 0)),
            pl.BlockSpec((tm, W), lambda i: (i, 0)),
            pl.BlockSpec((tm, 3 * W), lambda i: (i, 0)),
            pl.BlockSpec((tm, 2 * d), lambda i: (i, 0)),
            pl.BlockSpec((tm, 128), lambda i: (i, 0)),
        ],
        out_shape=[
            jax.ShapeDtypeStruct((n, 3 * W), BF16),
            jax.ShapeDtypeStruct((n, W), BF16),
            jax.ShapeDtypeStruct((n, 3 * W), BF16),
            jax.ShapeDtypeStruct((n, 2 * d), F32),
            jax.ShapeDtypeStruct((n, 128), F32),
        ],
        compiler_params=pltpu.CompilerParams(
            dimension_semantics=("arbitrary",), vmem_limit_bytes=VMEM_LIMIT_BYTES),
        name="pre_mix_proj",
    )(x2, g, w_packed)


def _sparse_kernel(qi_ref, kw_ref, wiT_ref, q_ref, k_ref, vT_ref, bias_ref, farb_ref, o_ref,
                   key_sc, msk_sc, oT_sc, *, tq, topk):
    kb = tq
    i = pl.program_id(1)
    t0 = i * tq
    nkb = i + 1
    row = lax.broadcasted_iota(jnp.int32, (kb, tq), 0)
    col = lax.broadcasted_iota(jnp.int32, (kb, tq), 1)
    w = wiT_ref[0] * (N_IDX_HEADS ** -0.5)

    def score_block(j, _):
        s0 = pl.multiple_of(j * kb, kb)
        kib = kw_ref[0, pl.ds(s0, kb), 0:IDX_DIM].astype(BF16)
        acc = jnp.zeros((kb, tq), F32)
        for h in range(N_IDX_HEADS):
            d = lax.dot_general(kib, qi_ref[0, :, h * IDX_DIM:(h + 1) * IDX_DIM], NT_DIMS,
                                preferred_element_type=F32)
            acc = acc + jnp.maximum(d, 0.0) * w[h:h + 1, :]
        adm = ((s0 + row) // CHUNK) <= ((t0 + col) // CHUNK)
        bits = pltpu.bitcast(acc, jnp.int32)
        key = bits ^ ((bits >> 31) & 0x7FFFFFFF)
        key = jnp.where(acc == 0.0, 0, key)
        key_sc[pl.ds(s0, kb), :] = jnp.where(adm, key, INT_MIN)
        return 0

    lax.fori_loop(0, nkb, score_block, 0)

    def count(pred):
        def body(j, cnt):
            s0 = pl.multiple_of(j * kb, kb)
            hit = pred(key_sc[pl.ds(s0, kb), :]).astype(jnp.int32)
            return cnt + jnp.sum(hit.reshape(kb // 8, 8, tq), axis=0)
        c8 = lax.fori_loop(0, nkb, body, jnp.zeros((8, tq), jnp.int32))
        return jnp.sum(c8, axis=0, keepdims=True)

    def bit_step(b, thr):
        cand = thr + lax.shift_left(jnp.int32(1), 31 - b)
        c = count(lambda blk: blk >= cand)
        return jnp.where(c >= topk, cand, thr)

    thr = lax.fori_loop(0, 32, bit_step, jnp.full((1, tq), INT_MIN, jnp.int32))
    need = (topk - count(lambda blk: blk > thr)).astype(F32)

    lower = (lax.broadcasted_iota(jnp.int32, (kb, kb), 1)
             < lax.broadcasted_iota(jnp.int32, (kb, kb), 0)).astype(BF16)

    def mask_block(j, tie_carry):
        s0 = pl.multiple_of(j * kb, kb)
        blk = key_sc[pl.ds(s0, kb), :]
        tie = blk == thr
        tie_f = jnp.where(tie, 1.0, 0.0)
        rank = jnp.dot(lower, tie_f.astype(BF16), preferred_element_type=F32) + tie_carry
        sel = ((blk > thr) | (tie & (rank < need))) & (blk != INT_MIN)
        msk_sc[pl.ds(s0, kb), :] = jnp.where(sel, 0.0, MASK_NEG)
        return tie_carry + jnp.sum(tie_f, axis=0, keepdims=True)

    lax.fori_loop(0, nkb, mask_block, jnp.zeros((1, tq), F32))

    for h in range(N_HEADS):
        hs = slice(h * HEAD_DIM, (h + 1) * HEAD_DIM)
        qh = q_ref[0, :, hs]

        def step(j, bias, carry, hs=hs, qh=qh):
            m, l, acc = carry
            s0 = pl.multiple_of(j * kb, kb)
            lg = lax.dot_general(k_ref[0, pl.ds(s0, kb), hs], qh, NT_DIMS,
                                 preferred_element_type=F32)
            lg = lg + bias + msk_sc[pl.ds(s0, kb), :]
            m_new = jnp.maximum(m, jnp.max(lg, axis=0, keepdims=True))
            p = jnp.exp(lg - m_new)
            alpha = jnp.exp(m - m_new)
            l = alpha * l + jnp.sum(p, axis=0, keepdims=True)
            acc = alpha * acc + jnp.dot(vT_ref[0, j, hs, :], p.astype(BF16),
                                        preferred_element_type=F32)
            return m_new, l, acc

        carry = (jnp.full((1, tq), M_INIT, F32), jnp.zeros((1, tq), F32),
                 jnp.zeros((HEAD_DIM, tq), F32))
        carry = lax.fori_loop(0, jnp.maximum(i - 1, 0),
                              lambda j, c, h=h, step=step: step(j, farb_ref[h], c), carry)
        carry = lax.cond(i >= 1,
                         lambda c, h=h, step=step: step(i - 1, bias_ref[h, 1], c),
                         lambda c: c, carry)
        m, l, acc = step(i, bias_ref[h, 0], carry)
        oT_sc[hs, :] = acc / l

    o_ref[0] = oT_sc[...].T.astype(BF16)


def _sparse(qi, kw, wiT, qkva, vT, bias_tiles, far_bias, tq, topk):
    b, s, _ = qi.shape
    W = W_ATT
    kern = functools.partial(_sparse_kernel, tq=tq, topk=topk)
    return pl.pallas_call(
        kern,
        grid=(b, s // tq),
        in_specs=[
            pl.BlockSpec((1, tq, W), lambda bi, i: (bi, i, 0)),
            pl.BlockSpec((1, s, 128), lambda bi, i: (bi, 0, 0)),
            pl.BlockSpec((1, N_IDX_HEADS, tq), lambda bi, i: (bi, 0, i)),
            pl.BlockSpec((1, tq, W), lambda bi, i: (bi, i, 0)),
            pl.BlockSpec((1, s, W), lambda bi, i: (bi, 0, 1)),
            pl.BlockSpec((1, s // tq, W, tq), lambda bi, i: (bi, 0, 0, 0)),
            pl.BlockSpec((N_HEADS, 2, tq, tq), lambda bi, i: (0, 0, 0, 0)),
            pl.BlockSpec((N_HEADS, 1, tq), lambda bi, i: (0, 0, 0)),
        ],
        out_specs=pl.BlockSpec((1, tq, W), lambda bi, i: (bi, i, 0)),
        out_shape=jax.ShapeDtypeStruct((b, s, W), BF16),
        scratch_shapes=[
            pltpu.VMEM((s, tq), jnp.int32),
            pltpu.VMEM((s, tq), F32),
            pltpu.VMEM((W, tq), F32),
        ],
        compiler_params=pltpu.CompilerParams(
            dimension_semantics=("arbitrary", "arbitrary"), vmem_limit_bytes=VMEM_LIMIT_BYTES),
        name="sparse_attention",
    )(qi, kw, wiT, qkva, qkva, vT, bias_tiles, far_bias)


def _sb_kernel(q_ref, k_ref, vT_ref, o_ref, accT_sc, *, tq, kb):
    i = pl.program_id(1)
    t0 = i * tq
    row = lax.broadcasted_iota(jnp.int32, (kb, tq), 0)
    col = lax.broadcasted_iota(jnp.int32, (kb, tq), 1)
    upper = (lax.broadcasted_iota(jnp.int32, (kb, kb), 1)
             >= lax.broadcasted_iota(jnp.int32, (kb, kb), 0)).astype(BF16)
    accT_sc[...] = jnp.zeros_like(accT_sc)

    def cond(st):
        j, carry = st
        return jnp.logical_and(j >= 0, jnp.max(carry) > SB_EXIT)

    def body(st):
        j, carry = st
        s0 = pl.multiple_of(j * kb, kb)
        valid = (s0 + row) < (t0 + col)
        rows = []
        for h in range(N_HEADS):
            hs = slice(h * HEAD_DIM, (h + 1) * HEAD_DIM)
            z = lax.dot_general(k_ref[0, pl.ds(s0, kb), hs], q_ref[0, :, hs], NT_DIMS,
                                preferred_element_type=F32)
            softplus = jnp.maximum(z, 0.0) + jnp.log1p(jnp.exp(-jnp.abs(z)))
            lk = jnp.where(valid, -softplus, 0.0)
            hi = lk.astype(BF16)
            lo = (lk - hi.astype(F32)).astype(BF16)
            suf = (jnp.dot(upper, hi, preferred_element_type=F32)
                   + jnp.dot(upper, lo, preferred_element_type=F32))
            c = carry[h:h + 1, :]
            a = jnp.where(valid, jnp.exp(z + suf + c), 0.0)
            accT_sc[hs, :] += jnp.dot(vT_ref[0, j, hs, :], a.astype(BF16),
                                      preferred_element_type=F32)
            rows.append(c + jnp.sum(lk, axis=0, keepdims=True))
        return j - 1, jnp.concatenate(rows, axis=0)

    j_top = (i + 1) * (tq // kb) - 1
    lax.while_loop(cond, body, (j_top, jnp.zeros((N_HEADS, tq), F32)))
    o_ref[0] = accT_sc[...].T.astype(BF16)


def _stick_breaking(qkvb, vT, tq, kb):
    b, s, _ = qkvb.shape
    W = W_ATT
    kern = functools.partial(_sb_kernel, tq=tq, kb=kb)
    return pl.pallas_call(
        kern,
        grid=(b, s // tq),
        in_specs=[
            pl.BlockSpec((1, tq, W), lambda bi, i: (bi, i, 0)),
            pl.BlockSpec((1, s, W), lambda bi, i: (bi, 0, 1)),
            pl.BlockSpec((1, s // kb, W, kb), lambda bi, i: (bi, 0, 0, 0)),
        ],
        out_specs=pl.BlockSpec((1, tq, W), lambda bi, i: (bi, i, 0)),
        out_shape=jax.ShapeDtypeStruct((b, s, W), BF16),
        scratch_shapes=[pltpu.VMEM((W, tq), F32)],
        compiler_params=pltpu.CompilerParams(
            dimension_semantics=("arbitrary", "arbitrary"), vmem_limit_bytes=VMEM_LIMIT_BYTES),
        name="stick_breaking_attention",
    )(qkvb, qkvb, vT)


def _merge_kernel(x_ref, oa_ref, ob_ref, gates_ref, wa_ref, wb_ref, wo_ref, g_ref, xo_ref):
    d = x_ref.shape[1]
    pa = jnp.dot(oa_ref[...], wa_ref[...], preferred_element_type=F32)
    pb = jnp.dot(ob_ref[...], wb_ref[...], preferred_element_type=F32)
    m = jax.nn.sigmoid(gates_ref[:, 0:d]) * pa + jax.nn.sigmoid(gates_ref[:, d:2 * d]) * pb
    y = jnp.dot(m.astype(BF16), wo_ref[...], preferred_element_type=F32)
    xo_ref[...] = x_ref[...] + _rms(y, g_ref[...])


def _merge(x2, oa, ob, gates, wa, wb, wo, g, tm):
    n, d = x2.shape
    W = W_ATT
    const = lambda i: (0, 0)
    return pl.pallas_call(
        _merge_kernel,
        grid=(n // tm,),
        in_specs=[
            pl.BlockSpec((tm, d), lambda i: (i, 0)),
            pl.BlockSpec((tm, W), lambda i: (i, 0)),
            pl.BlockSpec((tm, W), lambda i: (i, 0)),
            pl.BlockSpec((tm, 2 * d), lambda i: (i, 0)),
            pl.BlockSpec((W, d), const, pipeline_mode=pl.Buffered(1)),
            pl.BlockSpec((W, d), const, pipeline_mode=pl.Buffered(1)),
            pl.BlockSpec((d, d), const, pipeline_mode=pl.Buffered(1)),
            pl.BlockSpec((1, d), const),
        ],
        out_specs=pl.BlockSpec((tm, d), lambda i: (i, 0)),
        out_shape=jax.ShapeDtypeStruct((n, d), F32),
        compiler_params=pltpu.CompilerParams(
            dimension_semantics=("arbitrary",), vmem_limit_bytes=VMEM_LIMIT_BYTES),
        name="merge_out_proj",
    )(x2, oa, ob, gates, wa, wb, wo, g)


def _ffn_kernel(x_ref, gpre_ref, wgu_ref, wd_ref, gpost_ref, xo_ref, acc_sc, *, dff, fc):
    x = x_ref[...]
    h = _rms(x, gpre_ref[...]).astype(BF16)
    for c in range(0, dff, fc):
        gate = jnp.dot(h, wgu_ref[:, c:c + fc], preferred_element_type=F32)
        up = jnp.dot(h, wgu_ref[:, dff + c:dff + c + fc], preferred_element_type=F32)
        act = (jax.nn.silu(gate) * up).astype(BF16)
        part = jnp.dot(act, wd_ref[c:c + fc, :], preferred_element_type=F32)
        if c == 0:
            acc_sc[...] = part
        else:
            acc_sc[...] += part
    xo_ref[...] = x + _rms(acc_sc[...], gpost_ref[...])


def _ffn(x2, gpre, wgu, wd, gpost, tm, fc):
    n, d = x2.shape
    dff = wd.shape[0]
    const = lambda i: (0, 0)
    kern = functools.partial(_ffn_kernel, dff=dff, fc=fc)
    return pl.pallas_call(
        kern,
        grid=(n // tm,),
        in_specs=[
            pl.BlockSpec((tm, d), lambda i: (i, 0)),
            pl.BlockSpec((1, d), const),
            pl.BlockSpec((d, 2 * dff), const, pipeline_mode=pl.Buffered(1)),
            pl.BlockSpec((dff, d), const, pipeline_mode=pl.Buffered(1)),
            pl.BlockSpec((1, d), const),
        ],
        out_specs=pl.BlockSpec((tm, d), lambda i: (i, 0)),
        out_shape=jax.ShapeDtypeStruct((n, d), F32),
        scratch_shapes=[pltpu.VMEM((tm, d), F32)],
        compiler_params=pltpu.CompilerParams(
            dimension_semantics=("arbitrary",), vmem_limit_bytes=VMEM_LIMIT_BYTES),
        name="swiglu_ffn",
    )(x2, gpre, wgu, wd, gpost)


def _t5_bucket(rel):
    nb = N_BUCKETS // 2
    max_exact = nb // 2
    ret = (rel > 0).astype(jnp.int32) * nb
    n = jnp.abs(rel)
    nf = jnp.maximum(n, 1).astype(jnp.float32)
    large = max_exact + (jnp.log(nf / max_exact) / math.log(MAX_DISTANCE / max_exact)
                         * (nb - max_exact)).astype(jnp.int32)
    large = jnp.minimum(large, nb - 1)
    return ret + jnp.where(n < max_exact, n, large)


def _bias_tables(rel_bias, tq):
    s_l = jnp.arange(tq, dtype=jnp.int32)[:, None]
    t_l = jnp.arange(tq, dtype=jnp.int32)[None, :]
    rel = jnp.stack([s_l - t_l, s_l - t_l - tq])
    tiles = rel_bias.astype(F32)[_t5_bucket(rel)]
    tiles = tiles.transpose(3, 0, 1, 2)
    far = rel_bias.astype(F32)[_t5_bucket(jnp.full((tq,), -2 * tq, jnp.int32))]
    return tiles, far.T[:, None, :]


def _key_block_transpose(v, kb):
    b, s, w = v.shape
    return v.reshape(b, s // kb, kb, w).transpose(0, 1, 3, 2)


def _pack_w_in(w):
    d = w.shape[0]
    W = W_ATT
    o = 0
    qkva = w[:, o:o + 3 * W]; o += 3 * W
    qi = w[:, o:o + N_IDX_HEADS * IDX_DIM]; o += N_IDX_HEADS * IDX_DIM
    ki = w[:, o:o + IDX_DIM]; o += IDX_DIM
    wi = w[:, o:o + N_IDX_HEADS]; o += N_IDX_HEADS
    qkvb = w[:, o:o + 3 * W]; o += 3 * W
    gates = w[:, o:o + 2 * d]
    pad = jnp.zeros((d, 128 - IDX_DIM - N_IDX_HEADS), w.dtype)
    return jnp.concatenate([qkva, qi, qkvb, gates, ki, wi, pad], axis=1).astype(BF16)


def kernel(x, w_in, w_branch_sparse, w_branch_sb, w_out, w_gate_up, w_down,
           g_pre_mix, g_post_mix, g_pre_ffn, g_post_ffn, rel_bias):
    b, s, d = x.shape
    depth = w_in.shape[0]
    n = b * s
    W = W_ATT
    tm = min(512, n)
    tq = min(256, s)
    kb_sb = min(128, s)
    topk = min(INDEX_TOPK, s // 4)
    dff = w_down.shape[1]
    fc = 256
    assert s % tq == 0 and n % tm == 0 and dff % fc == 0 and tq % CHUNK == 0

    bias_tiles, far_bias = _bias_tables(rel_bias, tq)
    x2 = x.reshape(n, d)
    for l in range(depth):
        qkva, qi, qkvb, gates, kw = _proj(x2, g_pre_mix[l][None], _pack_w_in(w_in[l]), tm)
        qkva = qkva.reshape(b, s, 3 * W)
        qkvb = qkvb.reshape(b, s, 3 * W)
        kw3 = kw.reshape(b, s, 128)
        wiT = kw3[:, :, IDX_DIM:IDX_DIM + N_IDX_HEADS].transpose(0, 2, 1)
        o_a = _sparse(qi.reshape(b, s, W), kw3, wiT, qkva,
                      _key_block_transpose(qkva[:, :, 2 * W:], tq), bias_tiles, far_bias, tq, topk)
        o_b = _stick_breaking(qkvb, _key_block_transpose(qkvb[:, :, 2 * W:], kb_sb), tq, kb_sb)
        x2 = _merge(x2, o_a.reshape(n, W), o_b.reshape(n, W), gates,
                    w_branch_sparse[l].astype(BF16), w_branch_sb[l].astype(BF16),
                    w_out[l].astype(BF16), g_post_mix[l][None], tm)
        x2 = _ffn(x2, g_pre_ffn[l][None], w_gate_up[l].astype(BF16), w_down[l].astype(BF16),
                  g_post_ffn[l][None], tm, fc)
    return x2.reshape(b, s, d)
```

```python
import functools
import math

import jax
import jax.numpy as jnp
from jax import lax
from jax.experimental import pallas as pl
from jax.experimental.pallas import tpu as pltpu

F32 = jnp.float32
BF16 = jnp.bfloat16

CHUNK = 64
HEAD_DIM = 64
N_HEADS = 8
W_ATT = N_HEADS * HEAD_DIM
N_IDX_HEADS = 8
IDX_DIM = 64
INDEX_TOPK = 256
N_BUCKETS = 32
MAX_DISTANCE = 128
RMS_EPS = 1e-6

INT_MIN = -(2 ** 31)
MASK_NEG = -1e30
M_INIT = -1e29
SB_EXIT = -110.0

VMEM_LIMIT_BYTES = 58 * 1024 * 1024

NT_DIMS = (((1,), (1,)), ((), ()))


def _rms(x, g):
    ms = jnp.mean(x * x, axis=-1, keepdims=True)
    return x * lax.rsqrt(ms + RMS_EPS) * g


def _proj_kernel(x_ref, g_ref, w_ref, qkva_ref, qi_ref, qkvb_ref, gates_ref, kw_ref):
    h = _rms(x_ref[...], g_ref[...]).astype(BF16)

    def mm(c0, c1):
        return jnp.dot(h, w_ref[:, c0:c1], preferred_element_type=F32)

    W = W_ATT
    qkva_ref[:, 0:W] = (mm(0, W) * 0.125).astype(BF16)
    qkva_ref[:, W:2 * W] = mm(W, 2 * W).astype(BF16)
    qkva_ref[:, 2 * W:3 * W] = mm(2 * W, 3 * W).astype(BF16)
    qi_ref[...] = (mm(3 * W, 4 * W) * 0.125).astype(BF16)
    qkvb_ref[:, 0:W] = (mm(4 * W, 5 * W) * 0.125).astype(BF16)
    qkvb_ref[:, W:2 * W] = mm(5 * W, 6 * W).astype(BF16)
    qkvb_ref[:, 2 * W:3 * W] = mm(6 * W, 7 * W).astype(BF16)
    d = x_ref.shape[1]
    for c in range(0, 2 * d, W):
        gates_ref[:, c:c + W] = mm(7 * W + c, 7 * W + c + W)
    kw_ref[...] = mm(7 * W + 2 * d, 7 * W + 2 * d + 128)


def _proj(x2, g, w_packed, tm):
    n, d = x2.shape
    wcols = w_packed.shape[1]
    W = W_ATT
    return pl.pallas_call(
        _proj_kernel,
        grid=(n // tm,),
        in_specs=[
            pl.BlockSpec((tm, d), lambda i: (i, 0)),
            pl.BlockSpec((1, d), lambda i: (0, 0)),
            pl.BlockSpec((d, wcols), lambda i: (0, 0), pipeline_mode=pl.Buffered(1)),
        ],
        out_specs=[
            pl.BlockSpec((tm, 3 * W), lambda i: (i, 0)),
            pl.BlockSpec((tm, W), lambda i: (i, 0)),
            pl.BlockSpec((tm, 3 * W), lambda i: (i, 0)),
            pl.BlockSpec((tm, 2 * d), lambda i: (i, 0)),
            pl.BlockSpec((tm, 128), lambda i: (i, 0)),
        ],
        out_shape=[
            jax.ShapeDtypeStruct((n, 3 * W), BF16),
            jax.ShapeDtypeStruct((n, W), BF16),
            jax.ShapeDtypeStruct((n, 3 * W), BF16),
            jax.ShapeDtypeStruct((n, 2 * d), F32),
            jax.ShapeDtypeStruct((n, 128), F32),
        ],
        compiler_params=pltpu.CompilerParams(
            dimension_semantics=("arbitrary",), vmem_limit_bytes=VMEM_LIMIT_BYTES),
        name="pre_mix_proj",
    )(x2, g, w_packed)


def _sparse_kernel(qi_ref, kw_ref, wiT_ref, q_ref, k_ref, vT_ref, bias_ref, o_ref,
                   key_sc, hi_sc, lo_sc, lo2_sc, msk_sc, lg_sc, mb_sc, m_sc, l_sc, oT_sc, *, tq, topk):
    kb = tq
    i = pl.program_id(1)
    t0 = i * tq
    nkb = i + 1
    row = lax.broadcasted_iota(jnp.int32, (kb, tq), 0)
    col = lax.broadcasted_iota(jnp.int32, (kb, tq), 1)
    w = wiT_ref[0] * (N_IDX_HEADS ** -0.5)

    def blk_slice(j):
        return pl.ds(pl.multiple_of(j * kb, kb), kb)

    def score_block(j, _):
        ks = blk_slice(j)
        kib = kw_ref[0, ks, 0:IDX_DIM].astype(BF16)
        acc = jnp.zeros((kb, tq), F32)
        for h in range(N_IDX_HEADS):
            d = lax.dot_general(kib, qi_ref[0, :, h * IDX_DIM:(h + 1) * IDX_DIM], NT_DIMS,
                                preferred_element_type=F32)
            acc = acc + jnp.maximum(d, 0.0) * w[h:h + 1, :]
        adm = ((j * kb + row) // CHUNK) <= ((t0 + col) // CHUNK)
        bits = pltpu.bitcast(acc, jnp.int32)
        key = bits ^ ((bits >> 31) & 0x7FFFFFFF)
        key = jnp.where(acc == 0.0, 0, key)
        key = jnp.where(adm, key, INT_MIN)
        key_sc[ks, :] = key
        hi_sc[ks, :] = (key >> 16).astype(jnp.int16)
        lo_sc[ks, :] = ((key & 0xFFFF) - 32768).astype(jnp.int16)
        return 0

    lax.fori_loop(0, nkb, score_block, 0)

    def count16(ref, pred):
        one = jnp.ones((), BF16)
        zero = jnp.zeros((), BF16)

        def body(j, accs):
            hit = jnp.where(pred(ref[blk_slice(j), :]), one, zero).reshape(kb // 16, 16, tq)
            accs = list(accs)
            for r in range(kb // 16):
                accs[r % 4] = accs[r % 4] + hit[r]
            return tuple(accs)

        accs = lax.fori_loop(0, nkb, body, (jnp.zeros((16, tq), BF16),) * 4)
        tot = (accs[0].astype(F32) + accs[1].astype(F32)) + (accs[2].astype(F32) + accs[3].astype(F32))
        return jnp.sum(tot, axis=0, keepdims=True)

    def search16(ref, kth):
        def bit_step(b, thr):
            cand = thr + lax.shift_left(jnp.int32(1), 15 - b)
            cand16 = cand.astype(jnp.int16)
            c = count16(ref, lambda blk: blk >= cand16)
            return jnp.where(c >= kth, cand, thr)
        return lax.fori_loop(0, 16, bit_step, jnp.full((1, tq), -32768, jnp.int32))

    thr_hi = search16(hi_sc, float(topk))
    thr_hi16 = thr_hi.astype(jnp.int16)
    above_hi = count16(hi_sc, lambda blk: blk > thr_hi16)

    def low_half_block(j, _):
        ks = blk_slice(j)
        lo2_sc[ks, :] = jnp.where(hi_sc[ks, :] == thr_hi16, lo_sc[ks, :], jnp.int16(-32768))
        return 0

    lax.fori_loop(0, nkb, low_half_block, 0)
    thr_lo = search16(lo2_sc, topk - above_hi)
    thr_lo16 = thr_lo.astype(jnp.int16)
    need = topk - (above_hi + count16(lo2_sc, lambda blk: blk > thr_lo16))
    thr = thr_hi * 65536 + (thr_lo + 32768)

    def tie_count_block(j, cnt):
        tie_f = jnp.where(key_sc[blk_slice(j), :] == thr, 1.0, 0.0)
        return cnt + jnp.sum(tie_f.reshape(kb // 8, 8, tq), axis=0)

    ties = jnp.sum(lax.fori_loop(0, nkb, tie_count_block, jnp.zeros((8, tq), F32)),
                   axis=0, keepdims=True)
    spare = jnp.max(jnp.where((ties > need) & (thr != INT_MIN), 1.0, 0.0)) > 0.0

    def ranked_mask():
        lower = (lax.broadcasted_iota(jnp.int32, (kb, kb), 1)
                 < lax.broadcasted_iota(jnp.int32, (kb, kb), 0)).astype(BF16)

        def mask_block(j, tie_carry):
            ks = blk_slice(j)
            blk = key_sc[ks, :]
            tie = blk == thr
            tie_f = jnp.where(tie, 1.0, 0.0)
            rank = jnp.dot(lower, tie_f.astype(BF16), preferred_element_type=F32) + tie_carry
            sel = ((blk > thr) | (tie & (rank < need))) & (blk != INT_MIN)
            msk_sc[ks, :] = jnp.where(sel, 0.0, MASK_NEG)
            return tie_carry + jnp.sum(tie_f, axis=0, keepdims=True)

        lax.fori_loop(0, nkb, mask_block, jnp.zeros((1, tq), F32))

    def plain_mask():
        floor = jnp.maximum(thr, INT_MIN + 1)

        def mask_block(j, _):
            ks = blk_slice(j)
            msk_sc[ks, :] = jnp.where(key_sc[ks, :] >= floor, 0.0, MASK_NEG)
            return 0

        lax.fori_loop(0, nkb, mask_block, 0)

    lax.cond(spare, ranked_mask, plain_mask)

    oT_sc[...] = jnp.zeros_like(oT_sc)
    m_sc[...] = jnp.full((N_HEADS, tq), M_INIT, F32)
    l_sc[...] = jnp.zeros((N_HEADS, tq), F32)

    def attn_block(j, _):
        ks = blk_slice(j)
        d = jnp.minimum(i - j, 2)
        msk = msk_sc[ks, :]
        for h in range(N_HEADS):
            hs = slice(h * HEAD_DIM, (h + 1) * HEAD_DIM)
            lg = lax.dot_general(k_ref[0, ks, hs], q_ref[0, :, hs], NT_DIMS,
                                 preferred_element_type=F32)
            lg = lg + bias_ref[h, d] + msk
            lg_sc[h] = lg
            mb_sc[h:h + 1, :] = jnp.max(lg, axis=0, keepdims=True)

        @pl.when(j < nkb)
        def _():
            m_old = m_sc[...]
            m_new = jnp.maximum(m_old, mb_sc[...])
            alpha = jnp.exp(m_old - m_new)
            sums = []
            for h in range(N_HEADS):
                hs = slice(h * HEAD_DIM, (h + 1) * HEAD_DIM)
                p = jnp.exp(lg_sc[h] - m_new[h:h + 1, :])
                sums.append(jnp.sum(p, axis=0, keepdims=True))
                oT_sc[hs, :] = alpha[h:h + 1, :] * oT_sc[hs, :] + jnp.dot(
                    vT_ref[0, j, hs, :], p.astype(BF16), preferred_element_type=F32)
            m_sc[...] = m_new
            l_sc[...] = alpha * l_sc[...] + jnp.concatenate(sums, axis=0)

        return 0

    lax.fori_loop(0, nkb, attn_block, 0)
    l_all = l_sc[...]
    for h in range(N_HEADS):
        hs = slice(h * HEAD_DIM, (h + 1) * HEAD_DIM)
        oT_sc[hs, :] = oT_sc[hs, :] / l_all[h:h + 1, :]
    o_ref[0] = oT_sc[...].T.astype(BF16)


def _sparse(qi, kw, wiT, qkva, vT, bias_tiles, tq, topk):
    b, s, _ = qi.shape
    W = W_ATT
    kern = functools.partial(_sparse_kernel, tq=tq, topk=topk)
    return pl.pallas_call(
        kern,
        grid=(b, s // tq),
        in_specs=[
            pl.BlockSpec((1, tq, W), lambda bi, i: (bi, i, 0)),
            pl.BlockSpec((1, s, 128), lambda bi, i: (bi, 0, 0)),
            pl.BlockSpec((1, N_IDX_HEADS, tq), lambda bi, i: (bi, 0, i)),
            pl.BlockSpec((1, tq, W), lambda bi, i: (bi, i, 0)),
            pl.BlockSpec((1, s, W), lambda bi, i: (bi, 0, 1)),
            pl.BlockSpec((1, s // tq, W, tq), lambda bi, i: (bi, 0, 0, 0)),
            pl.BlockSpec((N_HEADS, 3, tq, tq), lambda bi, i: (0, 0, 0, 0),
                         pipeline_mode=pl.Buffered(1)),
        ],
        out_specs=pl.BlockSpec((1, tq, W), lambda bi, i: (bi, i, 0)),
        out_shape=jax.ShapeDtypeStruct((b, s, W), BF16),
        scratch_shapes=[
            pltpu.VMEM((s, tq), jnp.int32),
            pltpu.VMEM((s, tq), jnp.int16),
            pltpu.VMEM((s, tq), jnp.int16),
            pltpu.VMEM((s, tq), jnp.int16),
            pltpu.VMEM((s, tq), F32),
            pltpu.VMEM((N_HEADS, tq, tq), F32),
            pltpu.VMEM((N_HEADS, tq), F32),
            pltpu.VMEM((N_HEADS, tq), F32),
            pltpu.VMEM((N_HEADS, tq), F32),
            pltpu.VMEM((W, tq), F32),
        ],
        compiler_params=pltpu.CompilerParams(
            dimension_semantics=("arbitrary", "arbitrary"), vmem_limit_bytes=VMEM_LIMIT_BYTES),
        name="sparse_attention",
    )(qi, kw, wiT, qkva, qkva, vT, bias_tiles)


def _sb_kernel(q_ref, k_ref, vT_ref, o_ref, z_sc, hi_sc, lo_sc, a_sc, accT_sc, *, tq, kb):
    i = pl.program_id(1)
    t0 = i * tq
    nb = tq // kb
    row = lax.broadcasted_iota(jnp.int32, (kb, tq), 0)
    col = lax.broadcasted_iota(jnp.int32, (kb, tq), 1)
    upper = (lax.broadcasted_iota(jnp.int32, (kb, kb), 1)
             >= lax.broadcasted_iota(jnp.int32, (kb, kb), 0)).astype(BF16)
    accT_sc[...] = jnp.zeros_like(accT_sc)

    def block(j, carry, valid):
        ks = pl.ds(pl.multiple_of(j * kb, kb), kb)
        sums = []
        for h in range(N_HEADS):
            hs = slice(h * HEAD_DIM, (h + 1) * HEAD_DIM)
            z = lax.dot_general(k_ref[0, ks, hs], q_ref[0, :, hs], NT_DIMS,
                                preferred_element_type=F32)
            lk = -(jnp.maximum(z, 0.0) + jnp.log(1.0 + jnp.exp(-jnp.abs(z))))
            if valid is not None:
                lk = jnp.where(valid, lk, 0.0)
            hi = lk.astype(BF16)
            z_sc[h] = z
            hi_sc[h] = hi
            lo_sc[h] = (lk - hi.astype(F32)).astype(BF16)
            sums.append(jnp.sum(lk, axis=0, keepdims=True))

        @pl.when(j >= 0)
        def _():
            for h in range(N_HEADS):
                suf = (jnp.dot(upper, hi_sc[h], preferred_element_type=F32)
                       + jnp.dot(upper, lo_sc[h], preferred_element_type=F32))
                a = jnp.exp(z_sc[h] + suf + carry[h:h + 1, :])
                if valid is not None:
                    a = jnp.where(valid, a, 0.0)
                a_sc[h] = a.astype(BF16)

        @pl.when(j >= 0)
        def _():
            for h in range(N_HEADS):
                hs = slice(h * HEAD_DIM, (h + 1) * HEAD_DIM)
                accT_sc[hs, :] += jnp.dot(vT_ref[0, j, hs, :], a_sc[h], preferred_element_type=F32)

        return carry + jnp.concatenate(sums, axis=0)

    carry = jnp.zeros((N_HEADS, tq), F32)
    j_top = (i + 1) * nb - 1
    for dj in range(nb):
        j = j_top - dj
        carry = block(j, carry, (j * kb + row) < (t0 + col))

    def cond(st):
        j, carry = st
        return jnp.logical_and(j >= 0, jnp.max(carry) > SB_EXIT)

    lax.while_loop(cond, lambda st: (st[0] - 1, block(st[0], st[1], None)), (i * nb - 1, carry))
    o_ref[0] = accT_sc[...].T.astype(BF16)


def _stick_breaking(qkvb, vT, tq, kb):
    b, s, _ = qkvb.shape
    W = W_ATT
    kern = functools.partial(_sb_kernel, tq=tq, kb=kb)
    return pl.pallas_call(
        kern,
        grid=(b, s // tq),
        in_specs=[
            pl.BlockSpec((1, tq, W), lambda bi, i: (bi, i, 0)),
            pl.BlockSpec((1, s, W), lambda bi, i: (bi, 0, 1)),
            pl.BlockSpec((1, s // kb, W, kb), lambda bi, i: (bi, 0, 0, 0)),
        ],
        out_specs=pl.BlockSpec((1, tq, W), lambda bi, i: (bi, i, 0)),
        out_shape=jax.ShapeDtypeStruct((b, s, W), BF16),
        scratch_shapes=[
            pltpu.VMEM((N_HEADS, kb, tq), F32),
            pltpu.VMEM((N_HEADS, kb, tq), BF16),
            pltpu.VMEM((N_HEADS, kb, tq), BF16),
            pltpu.VMEM((N_HEADS, kb, tq), BF16),
            pltpu.VMEM((W, tq), F32),
        ],
        compiler_params=pltpu.CompilerParams(
            dimension_semantics=("arbitrary", "arbitrary"), vmem_limit_bytes=VMEM_LIMIT_BYTES),
        name="stick_breaking_attention",
    )(qkvb, qkvb, vT)


def _merge_kernel(x_ref, oa_ref, ob_ref, gates_ref, wa_ref, wb_ref, wo_ref, g_ref, xo_ref):
    d = x_ref.shape[1]
    pa = jnp.dot(oa_ref[...], wa_ref[...], preferred_element_type=F32)
    pb = jnp.dot(ob_ref[...], wb_ref[...], preferred_element_type=F32)
    m = jax.nn.sigmoid(gates_ref[:, 0:d]) * pa + jax.nn.sigmoid(gates_ref[:, d:2 * d]) * pb
    y = jnp.dot(m.astype(BF16), wo_ref[...], preferred_element_type=F32)
    xo_ref[...] = x_ref[...] + _rms(y, g_ref[...])


def _merge(x2, oa, ob, gates, wa, wb, wo, g, tm):
    n, d = x2.shape
    W = W_ATT
    const = lambda i: (0, 0)
    return pl.pallas_call(
        _merge_kernel,
        grid=(n // tm,),
        in_specs=[
            pl.BlockSpec((tm, d), lambda i: (i, 0)),
            pl.BlockSpec((tm, W), lambda i: (i, 0)),
            pl.BlockSpec((tm, W), lambda i: (i, 0)),
            pl.BlockSpec((tm, 2 * d), lambda i: (i, 0)),
            pl.BlockSpec((W, d), const, pipeline_mode=pl.Buffered(1)),
            pl.BlockSpec((W, d), const, pipeline_mode=pl.Buffered(1)),
            pl.BlockSpec((d, d), const, pipeline_mode=pl.Buffered(1)),
            pl.BlockSpec((1, d), const),
        ],
        out_specs=pl.BlockSpec((tm, d), lambda i: (i, 0)),
        out_shape=jax.ShapeDtypeStruct((n, d), F32),
        compiler_params=pltpu.CompilerParams(
            dimension_semantics=("arbitrary",), vmem_limit_bytes=VMEM_LIMIT_BYTES),
        name="merge_out_proj",
    )(x2, oa, ob, gates, wa, wb, wo, g)


def _ffn_kernel(x_ref, gpre_ref, wgu_ref, wd_ref, gpost_ref, xo_ref, acc_sc, *, dff, fc):
    x = x_ref[...]
    h = _rms(x, gpre_ref[...]).astype(BF16)
    for c in range(0, dff, fc):
        gate = jnp.dot(h, wgu_ref[:, c:c + fc], preferred_element_type=F32)
        up = jnp.dot(h, wgu_ref[:, dff + c:dff + c + fc], preferred_element_type=F32)
        act = (jax.nn.silu(gate) * up).astype(BF16)
        part = jnp.dot(act, wd_ref[c:c + fc, :], preferred_element_type=F32)
        if c == 0:
            acc_sc[...] = part
        else:
            acc_sc[...] += part
    xo_ref[...] = x + _rms(acc_sc[...], gpost_ref[...])


def _ffn(x2, gpre, wgu, wd, gpost, tm, fc):
    n, d = x2.shape
    dff = wd.shape[0]
    const = lambda i: (0, 0)
    kern = functools.partial(_ffn_kernel, dff=dff, fc=fc)
    return pl.pallas_call(
        kern,
        grid=(n // tm,),
        in_specs=[
            pl.BlockSpec((tm, d), lambda i: (i, 0)),
            pl.BlockSpec((1, d), const),
            pl.BlockSpec((d, 2 * dff), const, pipeline_mode=pl.Buffered(1)),
            pl.BlockSpec((dff, d), const, pipeline_mode=pl.Buffered(1)),
            pl.BlockSpec((1, d), const),
        ],
        out_specs=pl.BlockSpec((tm, d), lambda i: (i, 0)),
        out_shape=jax.ShapeDtypeStruct((n, d), F32),
        scratch_shapes=[pltpu.VMEM((tm, d), F32)],
        compiler_params=pltpu.CompilerParams(
            dimension_semantics=("arbitrary",), vmem_limit_bytes=VMEM_LIMIT_BYTES),
        name="swiglu_ffn",
    )(x2, gpre, wgu, wd, gpost)


def _t5_bucket(rel):
    nb = N_BUCKETS // 2
    max_exact = nb // 2
    ret = (rel > 0).astype(jnp.int32) * nb
    n = jnp.abs(rel)
    nf = jnp.maximum(n, 1).astype(jnp.float32)
    large = max_exact + (jnp.log(nf / max_exact) / math.log(MAX_DISTANCE / max_exact)
                         * (nb - max_exact)).astype(jnp.int32)
    large = jnp.minimum(large, nb - 1)
    return ret + jnp.where(n < max_exact, n, large)


def _bias_tables(rel_bias, tq):
    s_l = jnp.arange(tq, dtype=jnp.int32)[:, None]
    t_l = jnp.arange(tq, dtype=jnp.int32)[None, :]
    rel = jnp.stack([s_l - t_l - d * tq for d in range(3)])
    onehot = (_t5_bucket(rel)[..., None] == jnp.arange(N_BUCKETS)).astype(F32)
    return jnp.einsum("dstb,bh->hdst", onehot, rel_bias.astype(F32), precision=lax.Precision.HIGHEST)


def _key_block_transpose(v, kb):
    b, s, w = v.shape
    return v.reshape(b, s // kb, kb, w).transpose(0, 1, 3, 2)


def _pack_w_in(w):
    d = w.shape[0]
    W = W_ATT
    o = 0
    qkva = w[:, o:o + 3 * W]; o += 3 * W
    qi = w[:, o:o + N_IDX_HEADS * IDX_DIM]; o += N_IDX_HEADS * IDX_DIM
    ki = w[:, o:o + IDX_DIM]; o += IDX_DIM
    wi = w[:, o:o + N_IDX_HEADS]; o += N_IDX_HEADS
    qkvb = w[:, o:o + 3 * W]; o += 3 * W
    gates = w[:, o:o + 2 * d]
    pad = jnp.zeros((d, 128 - IDX_DIM - N_IDX_HEADS), w.dtype)
    return jnp.concatenate([qkva, qi, qkvb, gates, ki, wi, pad], axis=1).astype(BF16)


def kernel(x, w_in, w_branch_sparse, w_branch_sb, w_out, w_gate_up, w_down,
           g_pre_mix, g_post_mix, g_pre_ffn, g_post_ffn, rel_bias):
    b, s, d = x.shape
    depth = w_in.shape[0]
    n = b * s
    W = W_ATT
    tm = min(512, n)
    tq = min(256, s)
    kb_sb = min(128, s)
    topk = min(INDEX_TOPK, s // 4)
    dff = w_down.shape[1]
    fc = 256
    assert s % tq == 0 and n % tm == 0 and dff % fc == 0 and tq % CHUNK == 0

    bias_tiles = _bias_tables(rel_bias, tq)
    x2 = x.reshape(n, d)
    for l in range(depth):
        qkva, qi, qkvb, gates, kw = _proj(x2, g_pre_mix[l][None], _pack_w_in(w_in[l]), tm)
        qkva = qkva.reshape(b, s, 3 * W)
        qkvb = qkvb.reshape(b, s, 3 * W)
        kw3 = kw.reshape(b, s, 128)
        wiT = kw3[:, :, IDX_DIM:IDX_DIM + N_IDX_HEADS].transpose(0, 2, 1)
        o_a = _sparse(qi.reshape(b, s, W), kw3, wiT, qkva,
                      _key_block_transpose(qkva[:, :, 2 * W:], tq), bias_tiles, tq, topk)
        o_b = _stick_breaking(qkvb, _key_block_transpose(qkvb[:, :, 2 * W:], kb_sb), tq, kb_sb)
        x2 = _merge(x2, o_a.reshape(n, W), o_b.reshape(n, W), gates,
                    w_branch_sparse[l].astype(BF16), w_branch_sb[l].astype(BF16),
                    w_out[l].astype(BF16), g_post_mix[l][None], tm)
        x2 = _ffn(x2, g_pre_ffn[l][None], w_gate_up[l].astype(BF16), w_down[l].astype(BF16),
                  g_post_ffn[l][None], tm, fc)
    return x2.reshape(b, s, d)
```

```python
import functools
import math

import jax
import jax.numpy as jnp
from jax import lax
from jax.experimental import pallas as pl
from jax.experimental.pallas import tpu as pltpu

F32 = jnp.float32
BF16 = jnp.bfloat16

CHUNK = 64
HEAD_DIM = 64
N_HEADS = 8
W_ATT = N_HEADS * HEAD_DIM
N_IDX_HEADS = 8
IDX_DIM = 64
INDEX_TOPK = 256
N_BUCKETS = 32
MAX_DISTANCE = 128
RMS_EPS = 1e-6

INT_MIN = -(2 ** 31)
MASK_NEG = -1e30
M_INIT = -1e29
V_ROWS = HEAD_DIM + 16
SB_EXIT = -110.0

VMEM_LIMIT_BYTES = 58 * 1024 * 1024

NT_DIMS = (((1,), (1,)), ((), ()))


def _rms(x, g):
    ms = jnp.mean(x * x, axis=-1, keepdims=True)
    return x * lax.rsqrt(ms + RMS_EPS) * g


def _proj_kernel(x_ref, g_ref, w_ref, qka_ref, qi_ref, qkb_ref, gates_ref, kw_ref,
                 vTa_ref, vTb_ref, wiT_ref, *, tq):
    h = _rms(x_ref[...], g_ref[...]).astype(BF16)
    tm, d = x_ref.shape
    W = W_ATT

    def mm(c0, c1):
        return jnp.dot(h, w_ref[:, c0:c1], preferred_element_type=F32)

    qka_ref[:, 0:W] = (mm(0, W) * 0.125).astype(BF16)
    qka_ref[:, W:2 * W] = mm(W, 2 * W).astype(BF16)
    qi_ref[...] = (mm(3 * W, 4 * W) * 0.125).astype(BF16)
    qkb_ref[:, 0:W] = (mm(4 * W, 5 * W) * 0.125).astype(BF16)
    qkb_ref[:, W:2 * W] = mm(5 * W, 6 * W).astype(BF16)
    for c in range(0, 2 * d, W):
        gates_ref[:, c:c + W] = mm(7 * W + c, 7 * W + c + W)
    kw = mm(7 * W + 2 * d, 7 * W + 2 * d + 128)
    kw_ref[...] = kw
    wiT_ref[0] = kw.T[IDX_DIM:IDX_DIM + N_IDX_HEADS, :]

    vaT = mm(2 * W, 3 * W).T.astype(BF16)
    vbT = mm(6 * W, 7 * W).T.astype(BF16)
    tail = (lax.broadcasted_iota(jnp.int32, (V_ROWS - HEAD_DIM, tq), 0) == 0).astype(BF16)
    for blk in range(tm // tq):
        cols = slice(blk * tq, (blk + 1) * tq)
        vTb_ref[0, blk] = vbT[:, cols]
        for hd in range(N_HEADS):
            vTa_ref[0, blk, hd * V_ROWS:hd * V_ROWS + HEAD_DIM, :] = vaT[hd * HEAD_DIM:(hd + 1) * HEAD_DIM, cols]
            vTa_ref[0, blk, hd * V_ROWS + HEAD_DIM:(hd + 1) * V_ROWS, :] = tail


def _proj(x2, g, w_packed, tm, tq, s):
    n, d = x2.shape
    wcols = w_packed.shape[1]
    W = W_ATT
    tiles = s // tm
    nblk = tm // tq
    row = lambda i: (i, 0)
    seq = lambda i: (i // tiles, i % tiles, 0, 0)
    return pl.pallas_call(
        functools.partial(_proj_kernel, tq=tq),
        grid=(n // tm,),
        in_specs=[
            pl.BlockSpec((tm, d), row),
            pl.BlockSpec((1, d), lambda i: (0, 0)),
            pl.BlockSpec((d, wcols), lambda i: (0, 0), pipeline_mode=pl.Buffered(1)),
        ],
        out_specs=[
            pl.BlockSpec((tm, 2 * W), row),
            pl.BlockSpec((tm, W), row),
            pl.BlockSpec((tm, 2 * W), row),
            pl.BlockSpec((tm, 2 * d), row),
            pl.BlockSpec((tm, 128), row),
            pl.BlockSpec((1, nblk, N_HEADS * V_ROWS, tq), seq),
            pl.BlockSpec((1, nblk, W, tq), seq),
            pl.BlockSpec((1, N_IDX_HEADS, tm), lambda i: (i // tiles, 0, i % tiles)),
        ],
        out_shape=[
            jax.ShapeDtypeStruct((n, 2 * W), BF16),
            jax.ShapeDtypeStruct((n, W), BF16),
            jax.ShapeDtypeStruct((n, 2 * W), BF16),
            jax.ShapeDtypeStruct((n, 2 * d), F32),
            jax.ShapeDtypeStruct((n, 128), F32),
            jax.ShapeDtypeStruct((n // s, s // tq, N_HEADS * V_ROWS, tq), BF16),
            jax.ShapeDtypeStruct((n // s, s // tq, W, tq), BF16),
            jax.ShapeDtypeStruct((n // s, N_IDX_HEADS, s), F32),
        ],
        compiler_params=pltpu.CompilerParams(
            dimension_semantics=("arbitrary",), vmem_limit_bytes=VMEM_LIMIT_BYTES),
        name="pre_mix_proj",
    )(x2, g, w_packed)


def _sparse_kernel(qi_ref, kw_ref, wiT_ref, q_ref, k_ref, vT_ref, bias_ref, o_ref,
                   key_sc, hi_sc, lo_sc, lo2_sc, msk_sc, lg_sc, mb_sc, m_sc, oT_sc, on_sc, *, tq, topk):
    kb = tq
    i = pl.program_id(1)
    t0 = i * tq
    nkb = i + 1
    row = lax.broadcasted_iota(jnp.int32, (kb, tq), 0)
    col = lax.broadcasted_iota(jnp.int32, (kb, tq), 1)
    w = wiT_ref[0] * (N_IDX_HEADS ** -0.5)

    def blk_slice(j):
        return pl.ds(pl.multiple_of(j * kb, kb), kb)

    def score_block(j, _):
        ks = blk_slice(j)
        kib = kw_ref[0, ks, 0:IDX_DIM].astype(BF16)
        acc = jnp.zeros((kb, tq), F32)
        for h in range(N_IDX_HEADS):
            d = lax.dot_general(kib, qi_ref[0, :, h * IDX_DIM:(h + 1) * IDX_DIM], NT_DIMS,
                                preferred_element_type=F32)
            acc = acc + jnp.maximum(d, 0.0) * w[h:h + 1, :]
        adm = ((j * kb + row) // CHUNK) <= ((t0 + col) // CHUNK)
        bits = pltpu.bitcast(acc, jnp.int32)
        key = bits ^ ((bits >> 31) & 0x7FFFFFFF)
        key = jnp.where(acc == 0.0, 0, key)
        key = jnp.where(adm, key, INT_MIN)
        key_sc[ks, :] = key
        hi_sc[ks, :] = (key >> 16).astype(jnp.int16)
        lo_sc[ks, :] = ((key & 0xFFFF) - 32768).astype(jnp.int16)
        return 0

    lax.fori_loop(0, nkb, score_block, 0)

    def count16(ref, pred):
        one = jnp.ones((), BF16)
        zero = jnp.zeros((), BF16)

        def body(j, accs):
            hit = jnp.where(pred(ref[blk_slice(j), :]), one, zero).reshape(kb // 16, 16, tq)
            accs = list(accs)
            for r in range(kb // 16):
                accs[r % 4] = accs[r % 4] + hit[r]
            return tuple(accs)

        accs = lax.fori_loop(0, nkb, body, (jnp.zeros((16, tq), BF16),) * 4)
        tot = (accs[0].astype(F32) + accs[1].astype(F32)) + (accs[2].astype(F32) + accs[3].astype(F32))
        return jnp.sum(tot, axis=0, keepdims=True)

    def search16(ref, kth):
        def bit_step(b, thr):
            cand = thr + lax.shift_left(jnp.int32(1), 15 - b)
            cand16 = cand.astype(jnp.int16)
            c = count16(ref, lambda blk: blk >= cand16)
            return jnp.where(c >= kth, cand, thr)
        return lax.fori_loop(0, 16, bit_step, jnp.full((1, tq), -32768, jnp.int32))

    thr_hi = search16(hi_sc, float(topk))
    thr_hi16 = thr_hi.astype(jnp.int16)
    above_hi = count16(hi_sc, lambda blk: blk > thr_hi16)

    def low_half_block(j, _):
        ks = blk_slice(j)
        lo2_sc[ks, :] = jnp.where(hi_sc[ks, :] == thr_hi16, lo_sc[ks, :], jnp.int16(-32768))
        return 0

    lax.fori_loop(0, nkb, low_half_block, 0)
    thr_lo = search16(lo2_sc, topk - above_hi)
    thr_lo16 = thr_lo.astype(jnp.int16)
    need = topk - (above_hi + count16(lo2_sc, lambda blk: blk > thr_lo16))
    thr = thr_hi * 65536 + (thr_lo + 32768)

    def tie_count_block(j, cnt):
        tie_f = jnp.where(key_sc[blk_slice(j), :] == thr, 1.0, 0.0)
        return cnt + jnp.sum(tie_f.reshape(kb // 8, 8, tq), axis=0)

    ties = jnp.sum(lax.fori_loop(0, nkb, tie_count_block, jnp.zeros((8, tq), F32)),
                   axis=0, keepdims=True)
    spare = jnp.max(jnp.where((ties > need) & (thr != INT_MIN), 1.0, 0.0)) > 0.0

    def ranked_mask():
        lower = (lax.broadcasted_iota(jnp.int32, (kb, kb), 1)
                 < lax.broadcasted_iota(jnp.int32, (kb, kb), 0)).astype(BF16)

        def mask_block(j, tie_carry):
            ks = blk_slice(j)
            blk = key_sc[ks, :]
            tie = blk == thr
            tie_f = jnp.where(tie, 1.0, 0.0)
            rank = jnp.dot(lower, tie_f.astype(BF16), preferred_element_type=F32) + tie_carry
            sel = ((blk > thr) | (tie & (rank < need))) & (blk != INT_MIN)
            msk_sc[ks, :] = jnp.where(sel, 0.0, MASK_NEG)
            return tie_carry + jnp.sum(tie_f, axis=0, keepdims=True)

        lax.fori_loop(0, nkb, mask_block, jnp.zeros((1, tq), F32))

    def plain_mask():
        floor = jnp.maximum(thr, INT_MIN + 1)

        def mask_block(j, _):
            ks = blk_slice(j)
            msk_sc[ks, :] = jnp.where(key_sc[ks, :] >= floor, 0.0, MASK_NEG)
            return 0

        lax.fori_loop(0, nkb, mask_block, 0)

    lax.cond(spare, ranked_mask, plain_mask)

    oT_sc[...] = jnp.zeros_like(oT_sc)
    m_sc[...] = jnp.full((N_HEADS, tq), M_INIT, F32)

    def logits_stage(j, slot):
        ks = blk_slice(j)
        d = jnp.minimum(i - j, 2)
        msk = msk_sc[ks, :]
        for h in range(N_HEADS):
            hs = slice(h * HEAD_DIM, (h + 1) * HEAD_DIM)
            lg = lax.dot_general(k_ref[0, ks, hs], q_ref[0, :, hs], NT_DIMS,
                                 preferred_element_type=F32)
            lg = lg + bias_ref[h, d] + msk
            lg_sc[slot, h] = lg
            mb_sc[slot, h:h + 1, :] = jnp.max(lg, axis=0, keepdims=True)

    def softmax_stage(j, slot):
        m_old = m_sc[...]
        m_new = jnp.maximum(m_old, mb_sc[slot])
        alpha = jnp.exp(m_old - m_new)
        for h in range(N_HEADS):
            vs = slice(h * V_ROWS, (h + 1) * V_ROWS)
            p = jnp.exp(lg_sc[slot, h] - m_new[h:h + 1, :])
            oT_sc[vs, :] = alpha[h:h + 1, :] * oT_sc[vs, :] + jnp.dot(
                vT_ref[0, j, vs, :], p.astype(BF16), preferred_element_type=F32)
        m_sc[...] = m_new

    logits_stage(0, 0)

    def attn_block(j, _):
        slot = j & 1
        softmax_stage(j - 1, 1 - slot)
        logits_stage(j, slot)
        return 0

    lax.fori_loop(1, nkb, attn_block, 0)
    softmax_stage(nkb - 1, (nkb - 1) & 1)
    for h in range(N_HEADS):
        acc = oT_sc[h * V_ROWS:h * V_ROWS + HEAD_DIM, :]
        l = oT_sc[h * V_ROWS + HEAD_DIM:h * V_ROWS + HEAD_DIM + 1, :]
        on_sc[h * HEAD_DIM:(h + 1) * HEAD_DIM, :] = acc / l
    o_ref[0] = on_sc[...].T.astype(BF16)


def _sparse(qi, kw, wiT, qka, vT, bias_tiles, tq, topk):
    b, s, _ = qi.shape
    W = W_ATT
    kern = functools.partial(_sparse_kernel, tq=tq, topk=topk)
    return pl.pallas_call(
        kern,
        grid=(b, s // tq),
        in_specs=[
            pl.BlockSpec((1, tq, W), lambda bi, i: (bi, i, 0)),
            pl.BlockSpec((1, s, 128), lambda bi, i: (bi, 0, 0)),
            pl.BlockSpec((1, N_IDX_HEADS, tq), lambda bi, i: (bi, 0, i)),
            pl.BlockSpec((1, tq, W), lambda bi, i: (bi, i, 0)),
            pl.BlockSpec((1, s, W), lambda bi, i: (bi, 0, 1)),
            pl.BlockSpec((1, s // tq, N_HEADS * V_ROWS, tq), lambda bi, i: (bi, 0, 0, 0)),
            pl.BlockSpec((N_HEADS, 3, tq, tq), lambda bi, i: (0, 0, 0, 0),
                         pipeline_mode=pl.Buffered(1)),
        ],
        out_specs=pl.BlockSpec((1, tq, W), lambda bi, i: (bi, i, 0)),
        out_shape=jax.ShapeDtypeStruct((b, s, W), BF16),
        scratch_shapes=[
            pltpu.VMEM((s, tq), jnp.int32),
            pltpu.VMEM((s, tq), jnp.int16),
            pltpu.VMEM((s, tq), jnp.int16),
            pltpu.VMEM((s, tq), jnp.int16),
            pltpu.VMEM((s, tq), F32),
            pltpu.VMEM((2, N_HEADS, tq, tq), F32),
            pltpu.VMEM((2, N_HEADS, tq), F32),
            pltpu.VMEM((N_HEADS, tq), F32),
            pltpu.VMEM((N_HEADS * V_ROWS, tq), F32),
            pltpu.VMEM((W, tq), F32),
        ],
        compiler_params=pltpu.CompilerParams(
            dimension_semantics=("arbitrary", "arbitrary"), vmem_limit_bytes=VMEM_LIMIT_BYTES),
        name="sparse_attention",
    )(qi, kw, wiT, qka, qka, vT, bias_tiles)


def _sb_kernel(q_ref, k_ref, vT_ref, o_ref, z_sc, hi_sc, lo_sc, a_sc, accT_sc, *, tq):
    kb = tq
    i = pl.program_id(1)
    row = lax.broadcasted_iota(jnp.int32, (kb, tq), 0)
    col = lax.broadcasted_iota(jnp.int32, (kb, tq), 1)
    causal = row < col
    upper = (lax.broadcasted_iota(jnp.int32, (kb, kb), 1)
             >= lax.broadcasted_iota(jnp.int32, (kb, kb), 0)).astype(BF16)
    accT_sc[...] = jnp.zeros_like(accT_sc)

    def keep_stage(j, slot, diagonal):
        ks = pl.ds(pl.multiple_of(j * kb, kb), kb)
        sums = []
        for h in range(N_HEADS):
            hs = slice(h * HEAD_DIM, (h + 1) * HEAD_DIM)
            z = lax.dot_general(k_ref[0, ks, hs], q_ref[0, :, hs], NT_DIMS,
                                preferred_element_type=F32)
            lk = -(jnp.maximum(z, 0.0) + jnp.log(1.0 + jnp.exp(-jnp.abs(z))))
            if diagonal:
                lk = jnp.where(causal, lk, 0.0)
            hi = lk.astype(BF16)
            z_sc[slot, h] = z
            hi_sc[slot, h] = hi
            lo_sc[slot, h] = (lk - hi.astype(F32)).astype(BF16)
            sums.append(jnp.sum(lk, axis=0, keepdims=True))
        return jnp.concatenate(sums, axis=0)

    def weight_stage(slot, carry, diagonal):
        for h in range(N_HEADS):
            suf = (jnp.dot(upper, hi_sc[slot, h], preferred_element_type=F32)
                   + jnp.dot(upper, lo_sc[slot, h], preferred_element_type=F32))
            a = jnp.exp(z_sc[slot, h] + suf + carry[h:h + 1, :])
            if diagonal:
                a = jnp.where(causal, a, 0.0)
            a_sc[h] = a.astype(BF16)

    def value_stage(j):
        for h in range(N_HEADS):
            hs = slice(h * HEAD_DIM, (h + 1) * HEAD_DIM)
            accT_sc[hs, :] += jnp.dot(vT_ref[0, j, hs, :], a_sc[h], preferred_element_type=F32)

    zero = jnp.zeros((N_HEADS, tq), F32)
    c1 = keep_stage(i, 0, True)

    def only_block():
        weight_stage(0, zero, True)

        @pl.when(i == 0)
        def _():
            value_stage(i)

    def more_blocks():
        weight_stage(0, zero, True)
        c2 = c1 + keep_stage(i - 1, 1, False)

        @pl.when(i >= 1)
        def _():
            value_stage(i)

        def cond(st):
            j, _, _, c_next = st
            return jnp.logical_and(j >= 0, jnp.max(c_next) > SB_EXIT)

        def body(st):
            j, slot, c_pend, c_next = st
            weight_stage(slot, c_pend, False)
            sums = keep_stage(j, 1 - slot, False)

            @pl.when(j >= 0)
            def _():
                value_stage(j + 1)

            return j - 1, 1 - slot, c_next, c_next + sums

        j, slot, c_pend, _ = lax.while_loop(cond, body, (i - 2, jnp.int32(1), c1, c2))
        weight_stage(slot, c_pend, False)

        @pl.when(i >= 1)
        def _():
            value_stage(j + 1)

    lax.cond(i >= 1, more_blocks, only_block)
    o_ref[0] = accT_sc[...].T.astype(BF16)


def _stick_breaking(qkb, vT, tq):
    b, s, _ = qkb.shape
    W = W_ATT
    kern = functools.partial(_sb_kernel, tq=tq)
    return pl.pallas_call(
        kern,
        grid=(b, s // tq),
        in_specs=[
            pl.BlockSpec((1, tq, W), lambda bi, i: (bi, i, 0)),
            pl.BlockSpec((1, s, W), lambda bi, i: (bi, 0, 1)),
            pl.BlockSpec((1, s // tq, W, tq), lambda bi, i: (bi, 0, 0, 0)),
        ],
        out_specs=pl.BlockSpec((1, tq, W), lambda bi, i: (bi, i, 0)),
        out_shape=jax.ShapeDtypeStruct((b, s, W), BF16),
        scratch_shapes=[
            pltpu.VMEM((2, N_HEADS, tq, tq), F32),
            pltpu.VMEM((2, N_HEADS, tq, tq), BF16),
            pltpu.VMEM((2, N_HEADS, tq, tq), BF16),
            pltpu.VMEM((N_HEADS, tq, tq), BF16),
            pltpu.VMEM((W, tq), F32),
        ],
        compiler_params=pltpu.CompilerParams(
            dimension_semantics=("arbitrary", "arbitrary"), vmem_limit_bytes=VMEM_LIMIT_BYTES),
        name="stick_breaking_attention",
    )(qkb, qkb, vT)


def _merge_kernel(x_ref, oa_ref, ob_ref, gates_ref, wa_ref, wb_ref, wo_ref, g_ref, xo_ref):
    d = x_ref.shape[1]
    pa = jnp.dot(oa_ref[...], wa_ref[...], preferred_element_type=F32)
    pb = jnp.dot(ob_ref[...], wb_ref[...], preferred_element_type=F32)
    m = jax.nn.sigmoid(gates_ref[:, 0:d]) * pa + jax.nn.sigmoid(gates_ref[:, d:2 * d]) * pb
    y = jnp.dot(m.astype(BF16), wo_ref[...], preferred_element_type=F32)
    xo_ref[...] = x_ref[...] + _rms(y, g_ref[...])


def _merge(x2, oa, ob, gates, wa, wb, wo, g, tm):
    n, d = x2.shape
    W = W_ATT
    const = lambda i: (0, 0)
    return pl.pallas_call(
        _merge_kernel,
        grid=(n // tm,),
        in_specs=[
            pl.BlockSpec((tm, d), lambda i: (i, 0)),
            pl.BlockSpec((tm, W), lambda i: (i, 0)),
            pl.BlockSpec((tm, W), lambda i: (i, 0)),
            pl.BlockSpec((tm, 2 * d), lambda i: (i, 0)),
            pl.BlockSpec((W, d), const, pipeline_mode=pl.Buffered(1)),
            pl.BlockSpec((W, d), const, pipeline_mode=pl.Buffered(1)),
            pl.BlockSpec((d, d), const, pipeline_mode=pl.Buffered(1)),
            pl.BlockSpec((1, d), const),
        ],
        out_specs=pl.BlockSpec((tm, d), lambda i: (i, 0)),
        out_shape=jax.ShapeDtypeStruct((n, d), F32),
        compiler_params=pltpu.CompilerParams(
            dimension_semantics=("arbitrary",), vmem_limit_bytes=VMEM_LIMIT_BYTES),
        name="merge_out_proj",
    )(x2, oa, ob, gates, wa, wb, wo, g)


def _ffn_kernel(x_ref, gpre_ref, wgu_ref, wd_ref, gpost_ref, xo_ref, acc_sc, *, dff, fc):
    x = x_ref[...]
    h = _rms(x, gpre_ref[...]).astype(BF16)
    for c in range(0, dff, fc):
        gate = jnp.dot(h, wgu_ref[:, c:c + fc], preferred_element_type=F32)
        up = jnp.dot(h, wgu_ref[:, dff + c:dff + c + fc], preferred_element_type=F32)
        act = (jax.nn.silu(gate) * up).astype(BF16)
        part = jnp.dot(act, wd_ref[c:c + fc, :], preferred_element_type=F32)
        if c == 0:
            acc_sc[...] = part
        else:
            acc_sc[...] += part
    xo_ref[...] = x + _rms(acc_sc[...], gpost_ref[...])


def _ffn(x2, gpre, wgu, wd, gpost, tm, fc):
    n, d = x2.shape
    dff = wd.shape[0]
    const = lambda i: (0, 0)
    kern = functools.partial(_ffn_kernel, dff=dff, fc=fc)
    return pl.pallas_call(
        kern,
        grid=(n // tm,),
        in_specs=[
            pl.BlockSpec((tm, d), lambda i: (i, 0)),
            pl.BlockSpec((1, d), const),
            pl.BlockSpec((d, 2 * dff), const, pipeline_mode=pl.Buffered(1)),
            pl.BlockSpec((dff, d), const, pipeline_mode=pl.Buffered(1)),
            pl.BlockSpec((1, d), const),
        ],
        out_specs=pl.BlockSpec((tm, d), lambda i: (i, 0)),
        out_shape=jax.ShapeDtypeStruct((n, d), F32),
        scratch_shapes=[pltpu.VMEM((tm, d), F32)],
        compiler_params=pltpu.CompilerParams(
            dimension_semantics=("arbitrary",), vmem_limit_bytes=VMEM_LIMIT_BYTES),
        name="swiglu_ffn",
    )(x2, gpre, wgu, wd, gpost)


def _t5_bucket(rel):
    nb = N_BUCKETS // 2
    max_exact = nb // 2
    ret = (rel > 0).astype(jnp.int32) * nb
    n = jnp.abs(rel)
    nf = jnp.maximum(n, 1).astype(jnp.float32)
    large = max_exact + (jnp.log(nf / max_exact) / math.log(MAX_DISTANCE / max_exact)
                         * (nb - max_exact)).astype(jnp.int32)
    large = jnp.minimum(large, nb - 1)
    return ret + jnp.where(n < max_exact, n, large)


def _bias_tables(rel_bias, tq):
    s_l = jnp.arange(tq, dtype=jnp.int32)[:, None]
    t_l = jnp.arange(tq, dtype=jnp.int32)[None, :]
    rel = jnp.stack([s_l - t_l - d * tq for d in range(3)])
    onehot = (_t5_bucket(rel)[..., None] == jnp.arange(N_BUCKETS)).astype(F32)
    return jnp.einsum("dstb,bh->hdst", onehot, rel_bias.astype(F32), precision=lax.Precision.HIGHEST)


def _pack_w_in(w):
    d = w.shape[0]
    head = 3 * W_ATT + N_IDX_HEADS * IDX_DIM
    small = IDX_DIM + N_IDX_HEADS
    pad = jnp.zeros((d, 128 - small), w.dtype)
    return jnp.concatenate([w[:, :head], w[:, head + small:], w[:, head:head + small], pad],
                           axis=1).astype(BF16)


def kernel(x, w_in, w_branch_sparse, w_branch_sb, w_out, w_gate_up, w_down,
           g_pre_mix, g_post_mix, g_pre_ffn, g_post_ffn, rel_bias):
    b, s, d = x.shape
    depth = w_in.shape[0]
    n = b * s
    W = W_ATT
    tm = min(512, s)
    tq = min(256, s)
    topk = min(INDEX_TOPK, s // 4)
    dff = w_down.shape[1]
    fc = 256
    assert s % tm == 0 and tm % tq == 0 and dff % fc == 0 and tq % CHUNK == 0

    bias_tiles = _bias_tables(rel_bias, tq)
    x2 = x.reshape(n, d)
    for l in range(depth):
        qka, qi, qkb, gates, kw, vTa, vTb, wiT = _proj(x2, g_pre_mix[l][None], _pack_w_in(w_in[l]),
                                                        tm, tq, s)
        o_a = _sparse(qi.reshape(b, s, W), kw.reshape(b, s, 128), wiT, qka.reshape(b, s, 2 * W), vTa,
                      bias_tiles, tq, topk)
        o_b = _stick_breaking(qkb.reshape(b, s, 2 * W), vTb, tq)
        x2 = _merge(x2, o_a.reshape(n, W), o_b.reshape(n, W), gates,
                    w_branch_sparse[l].astype(BF16), w_branch_sb[l].astype(BF16),
                    w_out[l].astype(BF16), g_post_mix[l][None], tm)
        x2 = _ffn(x2, g_pre_ffn[l][None], w_gate_up[l].astype(BF16), w_down[l].astype(BF16),
                  g_post_ffn[l][None], tm, fc)
    return x2.reshape(b, s, d)
```

```python
import functools
import math

import jax
import jax.numpy as jnp
from jax import lax
from jax.experimental import pallas as pl
from jax.experimental.pallas import tpu as pltpu

F32 = jnp.float32
BF16 = jnp.bfloat16

CHUNK = 64
HEAD_DIM = 64
N_HEADS = 8
W_ATT = N_HEADS * HEAD_DIM
N_IDX_HEADS = 8
IDX_DIM = 64
INDEX_TOPK = 256
N_BUCKETS = 32
MAX_DISTANCE = 128
RMS_EPS = 1e-6

INT_MIN = -(2 ** 31)
MASK_NEG = -1e30
M_INIT = -1e29
V_ROWS = HEAD_DIM + 16
SB_EXIT = -110.0

VMEM_LIMIT_BYTES = 58 * 1024 * 1024


def _rms(x, g):
    ms = jnp.mean(x * x, axis=-1, keepdims=True)
    return x * lax.rsqrt(ms + RMS_EPS) * g


def _proj_kernel(x_ref, g_ref, w_ref, ka_ref, kb_ref, gates_ref, kw_ref,
                 qaT_ref, qiT_ref, qbT_ref, vTa_ref, vTb_ref, wiT_ref, *, tq):
    h = _rms(x_ref[...], g_ref[...]).astype(BF16)
    tm, d = x_ref.shape
    W = W_ATT

    def mm(c0, c1):
        return jnp.dot(h, w_ref[:, c0:c1], preferred_element_type=F32)

    qaT_ref[0] = (mm(0, W) * 0.125).T.astype(BF16)
    ka_ref[...] = mm(W, 2 * W).astype(BF16)
    qiT_ref[0] = (mm(3 * W, 4 * W) * 0.125).T.astype(BF16)
    qbT_ref[0] = (mm(4 * W, 5 * W) * 0.125).T.astype(BF16)
    kb_ref[...] = mm(5 * W, 6 * W).astype(BF16)
    for c in range(0, 2 * d, W):
        gates_ref[:, c:c + W] = mm(7 * W + c, 7 * W + c + W)
    kw = mm(7 * W + 2 * d, 7 * W + 2 * d + 128)
    kw_ref[...] = kw
    wiT_ref[0] = kw.T[IDX_DIM:IDX_DIM + N_IDX_HEADS, :]

    vaT = mm(2 * W, 3 * W).T.astype(BF16)
    vbT = mm(6 * W, 7 * W).T.astype(BF16)
    tail = (lax.broadcasted_iota(jnp.int32, (V_ROWS - HEAD_DIM, tq), 0) == 0).astype(BF16)
    for blk in range(tm // tq):
        cols = slice(blk * tq, (blk + 1) * tq)
        vTb_ref[0, blk] = vbT[:, cols]
        for hd in range(N_HEADS):
            vTa_ref[0, blk, hd * V_ROWS:hd * V_ROWS + HEAD_DIM, :] = vaT[hd * HEAD_DIM:(hd + 1) * HEAD_DIM, cols]
            vTa_ref[0, blk, hd * V_ROWS + HEAD_DIM:(hd + 1) * V_ROWS, :] = tail


def _proj(x2, g, w_packed, tm, tq, s):
    n, d = x2.shape
    wcols = w_packed.shape[1]
    W = W_ATT
    tiles = s // tm
    nblk = tm // tq
    row = lambda i: (i, 0)
    seq = lambda i: (i // tiles, i % tiles, 0, 0)
    colT = lambda i: (i // tiles, 0, i % tiles)
    return pl.pallas_call(
        functools.partial(_proj_kernel, tq=tq),
        grid=(n // tm,),
        in_specs=[
            pl.BlockSpec((tm, d), row),
            pl.BlockSpec((1, d), lambda i: (0, 0)),
            pl.BlockSpec((d, wcols), lambda i: (0, 0), pipeline_mode=pl.Buffered(1)),
        ],
        out_specs=[
            pl.BlockSpec((tm, W), row),
            pl.BlockSpec((tm, W), row),
            pl.BlockSpec((tm, 2 * d), row),
            pl.BlockSpec((tm, 128), row),
            pl.BlockSpec((1, W, tm), colT),
            pl.BlockSpec((1, W, tm), colT),
            pl.BlockSpec((1, W, tm), colT),
            pl.BlockSpec((1, nblk, N_HEADS * V_ROWS, tq), seq),
            pl.BlockSpec((1, nblk, W, tq), seq),
            pl.BlockSpec((1, N_IDX_HEADS, tm), colT),
        ],
        out_shape=[
            jax.ShapeDtypeStruct((n, W), BF16),
            jax.ShapeDtypeStruct((n, W), BF16),
            jax.ShapeDtypeStruct((n, 2 * d), F32),
            jax.ShapeDtypeStruct((n, 128), F32),
            jax.ShapeDtypeStruct((n // s, W, s), BF16),
            jax.ShapeDtypeStruct((n // s, W, s), BF16),
            jax.ShapeDtypeStruct((n // s, W, s), BF16),
            jax.ShapeDtypeStruct((n // s, s // tq, N_HEADS * V_ROWS, tq), BF16),
            jax.ShapeDtypeStruct((n // s, s // tq, W, tq), BF16),
            jax.ShapeDtypeStruct((n // s, N_IDX_HEADS, s), F32),
        ],
        compiler_params=pltpu.CompilerParams(
            dimension_semantics=("arbitrary",), vmem_limit_bytes=VMEM_LIMIT_BYTES),
        name="pre_mix_proj",
    )(x2, g, w_packed)


def _sparse_kernel(qiT_ref, kw_ref, wiT_ref, qT_ref, k_ref, vT_ref, bias_ref, o_ref,
                   key_sc, hi_sc, lo_sc, lo2_sc, msk_sc, lg_sc, mb_sc, m_sc, oT_sc, on_sc, *, tq, topk):
    kb = tq
    i = pl.program_id(1)
    t0 = i * tq
    nkb = i + 1
    row = lax.broadcasted_iota(jnp.int32, (kb, tq), 0)
    col = lax.broadcasted_iota(jnp.int32, (kb, tq), 1)
    w = wiT_ref[0] * (N_IDX_HEADS ** -0.5)

    def blk_slice(j):
        return pl.ds(pl.multiple_of(j * kb, kb), kb)

    def score_block(j, _):
        ks = blk_slice(j)
        kib = kw_ref[0, ks, 0:IDX_DIM].astype(BF16)
        acc = jnp.zeros((kb, tq), F32)
        for h in range(N_IDX_HEADS):
            d = jnp.dot(kib, qiT_ref[0, h * IDX_DIM:(h + 1) * IDX_DIM, :],
                        preferred_element_type=F32)
            acc = acc + jnp.maximum(d, 0.0) * w[h:h + 1, :]
        adm = ((j * kb + row) // CHUNK) <= ((t0 + col) // CHUNK)
        bits = pltpu.bitcast(acc, jnp.int32)
        key = bits ^ ((bits >> 31) & 0x7FFFFFFF)
        key = jnp.where(acc == 0.0, 0, key)
        key = jnp.where(adm, key, INT_MIN)
        key_sc[ks, :] = key
        hi_sc[ks, :] = (key >> 16).astype(jnp.int16)
        lo_sc[ks, :] = ((key & 0xFFFF) - 32768).astype(jnp.int16)
        return 0

    lax.fori_loop(0, nkb, score_block, 0)

    def count16(ref, pred):
        one = jnp.ones((), BF16)
        zero = jnp.zeros((), BF16)

        def body(j, accs):
            hit = jnp.where(pred(ref[blk_slice(j), :]), one, zero).reshape(kb // 16, 16, tq)
            accs = list(accs)
            for r in range(kb // 16):
                accs[r % 4] = accs[r % 4] + hit[r]
            return tuple(accs)

        accs = lax.fori_loop(0, nkb, body, (jnp.zeros((16, tq), BF16),) * 4)
        tot = (accs[0].astype(F32) + accs[1].astype(F32)) + (accs[2].astype(F32) + accs[3].astype(F32))
        return jnp.sum(tot, axis=0, keepdims=True)

    def search16(ref, kth):
        def bit_step(b, thr):
            cand = thr + lax.shift_left(jnp.int32(1), 15 - b)
            cand16 = cand.astype(jnp.int16)
            c = count16(ref, lambda blk: blk >= cand16)
            return jnp.where(c >= kth, cand, thr)
        return lax.fori_loop(0, 16, bit_step, jnp.full((1, tq), -32768, jnp.int32))

    thr_hi = search16(hi_sc, float(topk))
    thr_hi16 = thr_hi.astype(jnp.int16)

    def low_half_block(j, accs):
        ks = blk_slice(j)
        hi = hi_sc[ks, :]
        lo2_sc[ks, :] = jnp.where(hi == thr_hi16, lo_sc[ks, :], jnp.int16(-32768))
        hit = jnp.where(hi > thr_hi16, jnp.ones((), BF16), jnp.zeros((), BF16)).reshape(kb // 16, 16, tq)
        accs = list(accs)
        for r in range(kb // 16):
            accs[r % 4] = accs[r % 4] + hit[r]
        return tuple(accs)

    accs = lax.fori_loop(0, nkb, low_half_block, (jnp.zeros((16, tq), BF16),) * 4)
    above_hi = jnp.sum((accs[0].astype(F32) + accs[1].astype(F32))
                       + (accs[2].astype(F32) + accs[3].astype(F32)), axis=0, keepdims=True)
    thr_lo = search16(lo2_sc, topk - above_hi)
    thr_lo16 = thr_lo.astype(jnp.int16)
    need = topk - (above_hi + count16(lo2_sc, lambda blk: blk > thr_lo16))
    thr = thr_hi * 65536 + (thr_lo + 32768)

    floor = jnp.maximum(thr, INT_MIN + 1)

    def mask_block(j, cnt):
        ks = blk_slice(j)
        sel = key_sc[ks, :] >= floor
        msk_sc[ks, :] = jnp.where(sel, 0.0, MASK_NEG)
        return cnt + jnp.sum(jnp.where(sel, 1.0, 0.0).reshape(kb // 8, 8, tq), axis=0)

    taken = jnp.sum(lax.fori_loop(0, nkb, mask_block, jnp.zeros((8, tq), F32)), axis=0, keepdims=True)
    admissible = ((t0 + col[0:1, :]) // CHUNK + 1) * CHUNK
    spare = jnp.max(jnp.where(taken != jnp.minimum(topk, admissible).astype(F32), 1.0, 0.0)) > 0.0

    @pl.when(spare)
    def _():
        lower = (lax.broadcasted_iota(jnp.int32, (kb, kb), 1)
                 < lax.broadcasted_iota(jnp.int32, (kb, kb), 0)).astype(BF16)

        def ranked_block(j, tie_carry):
            ks = blk_slice(j)
            blk = key_sc[ks, :]
            tie = blk == thr
            tie_f = jnp.where(tie, 1.0, 0.0)
            rank = jnp.dot(lower, tie_f.astype(BF16), preferred_element_type=F32) + tie_carry
            sel = ((blk > thr) | (tie & (rank < need))) & (blk != INT_MIN)
            msk_sc[ks, :] = jnp.where(sel, 0.0, MASK_NEG)
            return tie_carry + jnp.sum(tie_f, axis=0, keepdims=True)

        lax.fori_loop(0, nkb, ranked_block, jnp.zeros((1, tq), F32))

    oT_sc[...] = jnp.zeros_like(oT_sc)
    m_sc[...] = jnp.full((N_HEADS, tq), M_INIT, F32)

    def logits_stage(j, slot):
        ks = blk_slice(j)
        d = jnp.minimum(i - j, 2)
        msk = msk_sc[ks, :]
        for h in range(N_HEADS):
            hs = slice(h * HEAD_DIM, (h + 1) * HEAD_DIM)
            lg = jnp.dot(k_ref[0, ks, hs], qT_ref[0, hs, :],
                         preferred_element_type=F32)
            lg = lg + bias_ref[h, d] + msk
            lg_sc[slot, h] = lg
            mb_sc[slot, h:h + 1, :] = jnp.max(lg, axis=0, keepdims=True)

    def softmax_stage(j, slot):
        m_old = m_sc[...]
        m_new = jnp.maximum(m_old, mb_sc[slot])
        alpha = jnp.exp(m_old - m_new)
        for h in range(N_HEADS):
            vs = slice(h * V_ROWS, (h + 1) * V_ROWS)
            p = jnp.exp(lg_sc[slot, h] - m_new[h:h + 1, :])
            oT_sc[vs, :] = alpha[h:h + 1, :] * oT_sc[vs, :] + jnp.dot(
                vT_ref[0, j, vs, :], p.astype(BF16), preferred_element_type=F32)
        m_sc[...] = m_new

    logits_stage(0, 0)

    def attn_block(j, _):
        slot = j & 1
        softmax_stage(j - 1, 1 - slot)
        logits_stage(j, slot)
        return 0

    lax.fori_loop(1, nkb, attn_block, 0)
    softmax_stage(nkb - 1, (nkb - 1) & 1)
    for h in range(N_HEADS):
        acc = oT_sc[h * V_ROWS:h * V_ROWS + HEAD_DIM, :]
        l = oT_sc[h * V_ROWS + HEAD_DIM:h * V_ROWS + HEAD_DIM + 1, :]
        on_sc[h * HEAD_DIM:(h + 1) * HEAD_DIM, :] = acc / l
    o_ref[0] = on_sc[...].T.astype(BF16)


def _sparse(qiT, kw, wiT, qT, k, vT, bias_tiles, tq, topk):
    b, s, _ = k.shape
    W = W_ATT
    kern = functools.partial(_sparse_kernel, tq=tq, topk=topk)
    return pl.pallas_call(
        kern,
        grid=(b, s // tq),
        in_specs=[
            pl.BlockSpec((1, W, tq), lambda bi, i: (bi, 0, i)),
            pl.BlockSpec((1, s, 128), lambda bi, i: (bi, 0, 0)),
            pl.BlockSpec((1, N_IDX_HEADS, tq), lambda bi, i: (bi, 0, i)),
            pl.BlockSpec((1, W, tq), lambda bi, i: (bi, 0, i)),
            pl.BlockSpec((1, s, W), lambda bi, i: (bi, 0, 0)),
            pl.BlockSpec((1, s // tq, N_HEADS * V_ROWS, tq), lambda bi, i: (bi, 0, 0, 0)),
            pl.BlockSpec((N_HEADS, 3, tq, tq), lambda bi, i: (0, 0, 0, 0),
                         pipeline_mode=pl.Buffered(1)),
        ],
        out_specs=pl.BlockSpec((1, tq, W), lambda bi, i: (bi, i, 0)),
        out_shape=jax.ShapeDtypeStruct((b, s, W), BF16),
        scratch_shapes=[
            pltpu.VMEM((s, tq), jnp.int32),
            pltpu.VMEM((s, tq), jnp.int16),
            pltpu.VMEM((s, tq), jnp.int16),
            pltpu.VMEM((s, tq), jnp.int16),
            pltpu.VMEM((s, tq), F32),
            pltpu.VMEM((2, N_HEADS, tq, tq), F32),
            pltpu.VMEM((2, N_HEADS, tq), F32),
            pltpu.VMEM((N_HEADS, tq), F32),
            pltpu.VMEM((N_HEADS * V_ROWS, tq), F32),
            pltpu.VMEM((W, tq), F32),
        ],
        compiler_params=pltpu.CompilerParams(
            dimension_semantics=("arbitrary", "arbitrary"), vmem_limit_bytes=VMEM_LIMIT_BYTES),
        name="sparse_attention",
    )(qiT, kw, wiT, qT, k, vT, bias_tiles)


def _sb_kernel(qT_ref, k_ref, vT_ref, o_ref, z_sc, hi_sc, lo_sc, a_sc, accT_sc, *, tq):
    kb = tq
    i = pl.program_id(1)
    row = lax.broadcasted_iota(jnp.int32, (kb, tq), 0)
    col = lax.broadcasted_iota(jnp.int32, (kb, tq), 1)
    causal = row < col
    upper = (lax.broadcasted_iota(jnp.int32, (kb, kb), 1)
             >= lax.broadcasted_iota(jnp.int32, (kb, kb), 0)).astype(BF16)
    accT_sc[...] = jnp.zeros_like(accT_sc)

    def keep_stage(j, slot, diagonal):
        ks = pl.ds(pl.multiple_of(j * kb, kb), kb)
        sums = []
        for h in range(N_HEADS):
            hs = slice(h * HEAD_DIM, (h + 1) * HEAD_DIM)
            z = jnp.dot(k_ref[0, ks, hs], qT_ref[0, hs, :],
                        preferred_element_type=F32)
            lk = -(jnp.maximum(z, 0.0) + jnp.log(1.0 + jnp.exp(-jnp.abs(z))))
            if diagonal:
                lk = jnp.where(causal, lk, 0.0)
            hi = lk.astype(BF16)
            z_sc[slot, h] = z
            hi_sc[slot, h] = hi
            lo_sc[slot, h] = (lk - hi.astype(F32)).astype(BF16)
            sums.append(jnp.sum(lk, axis=0, keepdims=True))
        return jnp.concatenate(sums, axis=0)

    def weight_stage(slot, carry, diagonal):
        for h in range(N_HEADS):
            suf = (jnp.dot(upper, hi_sc[slot, h], preferred_element_type=F32)
                   + jnp.dot(upper, lo_sc[slot, h], preferred_element_type=F32))
            a = jnp.exp(z_sc[slot, h] + suf + carry[h:h + 1, :])
            if diagonal:
                a = jnp.where(causal, a, 0.0)
            a_sc[h] = a.astype(BF16)

    def value_stage(j):
        for h in range(N_HEADS):
            hs = slice(h * HEAD_DIM, (h + 1) * HEAD_DIM)
            accT_sc[hs, :] += jnp.dot(vT_ref[0, j, hs, :], a_sc[h], preferred_element_type=F32)

    zero = jnp.zeros((N_HEADS, tq), F32)
    c1 = keep_stage(i, 0, True)

    def only_block():
        weight_stage(0, zero, True)

        @pl.when(i == 0)
        def _():
            value_stage(i)

    def more_blocks():
        weight_stage(0, zero, True)
        c2 = c1 + keep_stage(i - 1, 1, False)

        @pl.when(i >= 1)
        def _():
            value_stage(i)

        def cond(st):
            j, _, _, c_next = st
            return jnp.logical_and(j >= 0, jnp.max(c_next) > SB_EXIT)

        def body(st):
            j, slot, c_pend, c_next = st
            weight_stage(slot, c_pend, False)
            sums = keep_stage(j, 1 - slot, False)

            @pl.when(j >= 0)
            def _():
                value_stage(j + 1)

            return j - 1, 1 - slot, c_next, c_next + sums

        j, slot, c_pend, _ = lax.while_loop(cond, body, (i - 2, jnp.int32(1), c1, c2))
        weight_stage(slot, c_pend, False)

        @pl.when(i >= 1)
        def _():
            value_stage(j + 1)

    lax.cond(i >= 1, more_blocks, only_block)
    o_ref[0] = accT_sc[...].T.astype(BF16)


def _stick_breaking(qT, k, vT, tq):
    b, s, _ = k.shape
    W = W_ATT
    kern = functools.partial(_sb_kernel, tq=tq)
    return pl.pallas_call(
        kern,
        grid=(b, s // tq),
        in_specs=[
            pl.BlockSpec((1, W, tq), lambda bi, i: (bi, 0, i)),
            pl.BlockSpec((1, s, W), lambda bi, i: (bi, 0, 0)),
            pl.BlockSpec((1, s // tq, W, tq), lambda bi, i: (bi, 0, 0, 0)),
        ],
        out_specs=pl.BlockSpec((1, tq, W), lambda bi, i: (bi, i, 0)),
        out_shape=jax.ShapeDtypeStruct((b, s, W), BF16),
        scratch_shapes=[
            pltpu.VMEM((2, N_HEADS, tq, tq), F32),
            pltpu.VMEM((2, N_HEADS, tq, tq), BF16),
            pltpu.VMEM((2, N_HEADS, tq, tq), BF16),
            pltpu.VMEM((N_HEADS, tq, tq), BF16),
            pltpu.VMEM((W, tq), F32),
        ],
        compiler_params=pltpu.CompilerParams(
            dimension_semantics=("arbitrary", "arbitrary"), vmem_limit_bytes=VMEM_LIMIT_BYTES),
        name="stick_breaking_attention",
    )(qT, k, vT)


def _merge_ffn_kernel(x_ref, oa_ref, ob_ref, gates_ref, wa_ref, wb_ref, wo_ref, gmix_ref,
                      gpre_ref, wgu_ref, wd_ref, gpost_ref, xo_ref, x1_sc, acc_sc, *, dff, fc):
    d = x_ref.shape[1]
    pa = jnp.dot(oa_ref[...], wa_ref[...], preferred_element_type=F32)
    pb = jnp.dot(ob_ref[...], wb_ref[...], preferred_element_type=F32)
    m = jax.nn.sigmoid(gates_ref[:, 0:d]) * pa + jax.nn.sigmoid(gates_ref[:, d:2 * d]) * pb
    y = jnp.dot(m.astype(BF16), wo_ref[...], preferred_element_type=F32)
    x1_sc[...] = x_ref[...] + _rms(y, gmix_ref[...])
    h = _rms(x1_sc[...], gpre_ref[...]).astype(BF16)
    for c in range(0, dff, fc):
        gate = jnp.dot(h, wgu_ref[:, c:c + fc], preferred_element_type=F32)
        up = jnp.dot(h, wgu_ref[:, dff + c:dff + c + fc], preferred_element_type=F32)
        act = (jax.nn.silu(gate) * up).astype(BF16)
        part = jnp.dot(act, wd_ref[c:c + fc, :], preferred_element_type=F32)
        if c == 0:
            acc_sc[...] = part
        else:
            acc_sc[...] += part
    xo_ref[...] = x1_sc[...] + _rms(acc_sc[...], gpost_ref[...])


def _merge_ffn(x2, oa, ob, gates, wa, wb, wo, gmix, gpre, wgu, wd, gpost, tm, fc):
    n, d = x2.shape
    W = W_ATT
    dff = wd.shape[0]
    row = lambda i: (i, 0)
    const = lambda i: (0, 0)
    resident = functools.partial(pl.BlockSpec, index_map=const, pipeline_mode=pl.Buffered(1))
    kern = functools.partial(_merge_ffn_kernel, dff=dff, fc=fc)
    return pl.pallas_call(
        kern,
        grid=(n // tm,),
        in_specs=[
            pl.BlockSpec((tm, d), row),
            pl.BlockSpec((tm, W), row),
            pl.BlockSpec((tm, W), row),
            pl.BlockSpec((tm, 2 * d), row),
            resident((W, d)),
            resident((W, d)),
            resident((d, d)),
            pl.BlockSpec((1, d), const),
            pl.BlockSpec((1, d), const),
            resident((d, 2 * dff)),
            resident((dff, d)),
            pl.BlockSpec((1, d), const),
        ],
        out_specs=pl.BlockSpec((tm, d), row),
        out_shape=jax.ShapeDtypeStruct((n, d), F32),
        scratch_shapes=[pltpu.VMEM((tm, d), F32), pltpu.VMEM((tm, d), F32)],
        compiler_params=pltpu.CompilerParams(
            dimension_semantics=("arbitrary",), vmem_limit_bytes=VMEM_LIMIT_BYTES),
        name="merge_swiglu",
    )(x2, oa, ob, gates, wa, wb, wo, gmix, gpre, wgu, wd, gpost)


def _t5_bucket(rel):
    nb = N_BUCKETS // 2
    max_exact = nb // 2
    ret = (rel > 0).astype(jnp.int32) * nb
    n = jnp.abs(rel)
    nf = jnp.maximum(n, 1).astype(jnp.float32)
    large = max_exact + (jnp.log(nf / max_exact) / math.log(MAX_DISTANCE / max_exact)
                         * (nb - max_exact)).astype(jnp.int32)
    large = jnp.minimum(large, nb - 1)
    return ret + jnp.where(n < max_exact, n, large)


def _bias_tables(rel_bias, tq):
    s_l = jnp.arange(tq, dtype=jnp.int32)[:, None]
    t_l = jnp.arange(tq, dtype=jnp.int32)[None, :]
    rel = jnp.stack([s_l - t_l - d * tq for d in range(3)])
    onehot = (_t5_bucket(rel)[..., None] == jnp.arange(N_BUCKETS)).astype(F32)
    return jnp.einsum("dstb,bh->hdst", onehot, rel_bias.astype(F32), precision=lax.Precision.HIGHEST)


def _pack_w_in(w):
    d = w.shape[0]
    head = 3 * W_ATT + N_IDX_HEADS * IDX_DIM
    small = IDX_DIM + N_IDX_HEADS
    pad = jnp.zeros((d, 128 - small), w.dtype)
    return jnp.concatenate([w[:, :head], w[:, head + small:], w[:, head:head + small], pad],
                           axis=1).astype(BF16)


def kernel(x, w_in, w_branch_sparse, w_branch_sb, w_out, w_gate_up, w_down,
           g_pre_mix, g_post_mix, g_pre_ffn, g_post_ffn, rel_bias):
    b, s, d = x.shape
    depth = w_in.shape[0]
    n = b * s
    W = W_ATT
    tm = min(512, s)
    tq = min(256, s)
    topk = min(INDEX_TOPK, s // 4)
    dff = w_down.shape[1]
    fc = 256
    assert s % tm == 0 and tm % tq == 0 and dff % fc == 0 and tq % CHUNK == 0

    bias_tiles = _bias_tables(rel_bias, tq)
    x2 = x.reshape(n, d)
    for l in range(depth):
        ka, kb, gates, kw, qaT, qiT, qbT, vTa, vTb, wiT = _proj(
            x2, g_pre_mix[l][None], _pack_w_in(w_in[l]), tm, tq, s)
        o_a = _sparse(qiT, kw.reshape(b, s, 128), wiT, qaT, ka.reshape(b, s, W), vTa,
                      bias_tiles, tq, topk)
        o_b = _stick_breaking(qbT, kb.reshape(b, s, W), vTb, tq)
        x2 = _merge_ffn(x2, o_a.reshape(n, W), o_b.reshape(n, W), gates,
                        w_branch_sparse[l].astype(BF16), w_branch_sb[l].astype(BF16),
                        w_out[l].astype(BF16), g_post_mix[l][None], g_pre_ffn[l][None],
                        w_gate_up[l].astype(BF16), w_down[l].astype(BF16), g_post_ffn[l][None], tm, fc)
    return x2.reshape(b, s, d)
```

```python
import functools
import math

import jax
import jax.numpy as jnp
from jax import lax
from jax.experimental import pallas as pl
from jax.experimental.pallas import tpu as pltpu

F32 = jnp.float32
BF16 = jnp.bfloat16

CHUNK = 64
HEAD_DIM = 64
N_HEADS = 8
W_ATT = N_HEADS * HEAD_DIM
N_IDX_HEADS = 8
IDX_DIM = 64
INDEX_TOPK = 256
N_BUCKETS = 32
MAX_DISTANCE = 128
RMS_EPS = 1e-6

INT_MIN = -(2 ** 31)
MASK_NEG = -1e30
M_INIT = -1e29
V_ROWS = HEAD_DIM + 16
SB_EXIT = -88.0

VMEM_LIMIT_BYTES = 58 * 1024 * 1024


def _rms(x, g):
    ms = jnp.mean(x * x, axis=-1, keepdims=True)
    return x * lax.rsqrt(ms + RMS_EPS) * g


def _proj_kernel(x_ref, g_ref, w_ref, ka_ref, kb_ref, gates_ref, kw_ref,
                 qaT_ref, qiT_ref, qbT_ref, vTa_ref, vTb_ref, wiT_ref, *, tq, kb_sb):
    h = _rms(x_ref[...], g_ref[...]).astype(BF16)
    tm, d = x_ref.shape
    W = W_ATT

    def mm(c0, c1):
        return jnp.dot(h, w_ref[:, c0:c1], preferred_element_type=F32)

    qaT_ref[0] = (mm(0, W) * 0.125).T.astype(BF16)
    ka_ref[...] = mm(W, 2 * W).astype(BF16)
    qiT_ref[0] = (mm(3 * W, 4 * W) * 0.125).T.astype(BF16)
    qbT_ref[0] = (mm(4 * W, 5 * W) * 0.125).T.astype(BF16)
    kb_ref[...] = mm(5 * W, 6 * W).astype(BF16)
    for c in range(0, 2 * d, W):
        gates_ref[:, c:c + W] = mm(7 * W + c, 7 * W + c + W)
    kw = mm(7 * W + 2 * d, 7 * W + 2 * d + 128)
    kw_ref[...] = kw
    wiT_ref[0] = kw.T[IDX_DIM:IDX_DIM + N_IDX_HEADS, :]

    vaT = mm(2 * W, 3 * W).T.astype(BF16)
    vbT = mm(6 * W, 7 * W).T.astype(BF16)
    tail = (lax.broadcasted_iota(jnp.int32, (V_ROWS - HEAD_DIM, tq), 0) == 0).astype(BF16)
    for blk in range(tm // kb_sb):
        vTb_ref[0, blk] = vbT[:, blk * kb_sb:(blk + 1) * kb_sb]
    for blk in range(tm // tq):
        cols = slice(blk * tq, (blk + 1) * tq)
        for hd in range(N_HEADS):
            vTa_ref[0, blk, hd * V_ROWS:hd * V_ROWS + HEAD_DIM, :] = vaT[hd * HEAD_DIM:(hd + 1) * HEAD_DIM, cols]
            vTa_ref[0, blk, hd * V_ROWS + HEAD_DIM:(hd + 1) * V_ROWS, :] = tail


def _proj(x2, g, w_packed, tm, tq, kb_sb, s):
    n, d = x2.shape
    wcols = w_packed.shape[1]
    W = W_ATT
    tiles = s // tm
    nblk = tm // tq
    row = lambda i: (i, 0)
    seq = lambda i: (i // tiles, i % tiles, 0, 0)
    colT = lambda i: (i // tiles, 0, i % tiles)
    return pl.pallas_call(
        functools.partial(_proj_kernel, tq=tq, kb_sb=kb_sb),
        grid=(n // tm,),
        in_specs=[
            pl.BlockSpec((tm, d), row),
            pl.BlockSpec((1, d), lambda i: (0, 0)),
            pl.BlockSpec((d, wcols), lambda i: (0, 0), pipeline_mode=pl.Buffered(1)),
        ],
        out_specs=[
            pl.BlockSpec((tm, W), row),
            pl.BlockSpec((tm, W), row),
            pl.BlockSpec((tm, 2 * d), row),
            pl.BlockSpec((tm, 128), row),
            pl.BlockSpec((1, W, tm), colT),
            pl.BlockSpec((1, W, tm), colT),
            pl.BlockSpec((1, W, tm), colT),
            pl.BlockSpec((1, nblk, N_HEADS * V_ROWS, tq), seq),
            pl.BlockSpec((1, tm // kb_sb, W, kb_sb), seq),
            pl.BlockSpec((1, N_IDX_HEADS, tm), colT),
        ],
        out_shape=[
            jax.ShapeDtypeStruct((n, W), BF16),
            jax.ShapeDtypeStruct((n, W), BF16),
            jax.ShapeDtypeStruct((n, 2 * d), F32),
            jax.ShapeDtypeStruct((n, 128), F32),
            jax.ShapeDtypeStruct((n // s, W, s), BF16),
            jax.ShapeDtypeStruct((n // s, W, s), BF16),
            jax.ShapeDtypeStruct((n // s, W, s), BF16),
            jax.ShapeDtypeStruct((n // s, s // tq, N_HEADS * V_ROWS, tq), BF16),
            jax.ShapeDtypeStruct((n // s, s // kb_sb, W, kb_sb), BF16),
            jax.ShapeDtypeStruct((n // s, N_IDX_HEADS, s), F32),
        ],
        compiler_params=pltpu.CompilerParams(
            dimension_semantics=("arbitrary",), vmem_limit_bytes=VMEM_LIMIT_BYTES),
        name="pre_mix_proj",
    )(x2, g, w_packed)


def _sparse_kernel(qiT_ref, kw_ref, wiT_ref, qT_ref, k_ref, vT_ref, bias_ref, o_ref,
                   key_sc, hi_sc, lo_sc, lo2_sc, msk_sc, lg_sc, mb_sc, m_sc, oT_sc, on_sc, *, tq, topk):
    kb = tq
    i = pl.program_id(1)
    t0 = i * tq
    nkb = i + 1
    row = lax.broadcasted_iota(jnp.int32, (kb, tq), 0)
    col = lax.broadcasted_iota(jnp.int32, (kb, tq), 1)
    w = wiT_ref[0] * (N_IDX_HEADS ** -0.5)

    def blk_slice(j):
        return pl.ds(pl.multiple_of(j * kb, kb), kb)

    def score_block(j):
        ks = blk_slice(j)
        kib = kw_ref[0, ks, 0:IDX_DIM].astype(BF16)
        acc = jnp.zeros((kb, tq), F32)
        for h in range(N_IDX_HEADS):
            d = jnp.dot(kib, qiT_ref[0, h * IDX_DIM:(h + 1) * IDX_DIM, :],
                        preferred_element_type=F32)
            acc = acc + jnp.maximum(d, 0.0) * w[h:h + 1, :]
        adm = ((j * kb + row) // CHUNK) <= ((t0 + col) // CHUNK)
        bits = pltpu.bitcast(acc, jnp.int32)
        key = bits ^ ((bits >> 31) & 0x7FFFFFFF)
        key = jnp.where(acc == 0.0, 0, key)
        key = jnp.where(adm, key, INT_MIN)
        key_sc[ks, :] = key
        hi_sc[ks, :] = (key >> 16).astype(jnp.int16)
        lo_sc[ks, :] = ((key & 0xFFFF) - 32768).astype(jnp.int16)

    def score_pair(jj, _):
        score_block(2 * jj)
        score_block(jnp.minimum(2 * jj + 1, nkb - 1))
        return 0

    lax.fori_loop(0, (nkb + 1) // 2, score_pair, 0)

    def count16(ref, pred):
        one = jnp.ones((), BF16)
        zero = jnp.zeros((), BF16)

        def body(j, accs):
            hit = jnp.where(pred(ref[blk_slice(j), :]), one, zero).reshape(kb // 16, 16, tq)
            accs = list(accs)
            for r in range(kb // 16):
                accs[r % 4] = accs[r % 4] + hit[r]
            return tuple(accs)

        accs = lax.fori_loop(0, nkb, body, (jnp.zeros((16, tq), BF16),) * 4)
        tot = (accs[0].astype(F32) + accs[1].astype(F32)) + (accs[2].astype(F32) + accs[3].astype(F32))
        return jnp.sum(tot, axis=0, keepdims=True)

    def search16(ref, kth):
        def bit_step(b, thr):
            cand = thr + lax.shift_left(jnp.int32(1), 15 - b)
            cand16 = cand.astype(jnp.int16)
            c = count16(ref, lambda blk: blk >= cand16)
            return jnp.where(c >= kth, cand, thr)
        return lax.fori_loop(0, 16, bit_step, jnp.full((1, tq), -32768, jnp.int32))

    thr_hi = search16(hi_sc, float(topk))
    thr_hi16 = thr_hi.astype(jnp.int16)

    def low_half_block(j, accs):
        ks = blk_slice(j)
        hi = hi_sc[ks, :]
        lo2_sc[ks, :] = jnp.where(hi == thr_hi16, lo_sc[ks, :], jnp.int16(-32768))
        hit = jnp.where(hi > thr_hi16, jnp.ones((), BF16), jnp.zeros((), BF16)).reshape(kb // 16, 16, tq)
        accs = list(accs)
        for r in range(kb // 16):
            accs[r % 4] = accs[r % 4] + hit[r]
        return tuple(accs)

    accs = lax.fori_loop(0, nkb, low_half_block, (jnp.zeros((16, tq), BF16),) * 4)
    above_hi = jnp.sum((accs[0].astype(F32) + accs[1].astype(F32))
                       + (accs[2].astype(F32) + accs[3].astype(F32)), axis=0, keepdims=True)
    thr_lo = search16(lo2_sc, topk - above_hi)
    thr_lo16 = thr_lo.astype(jnp.int16)
    need = topk - (above_hi + count16(lo2_sc, lambda blk: blk > thr_lo16))
    thr = thr_hi * 65536 + (thr_lo + 32768)

    floor = jnp.maximum(thr, INT_MIN + 1)

    def mask_block(j, cnt):
        ks = blk_slice(j)
        sel = key_sc[ks, :] >= floor
        msk_sc[ks, :] = jnp.where(sel, 0.0, MASK_NEG)
        return cnt + jnp.sum(jnp.where(sel, 1.0, 0.0).reshape(kb // 8, 8, tq), axis=0)

    taken = jnp.sum(lax.fori_loop(0, nkb, mask_block, jnp.zeros((8, tq), F32)), axis=0, keepdims=True)
    admissible = ((t0 + col[0:1, :]) // CHUNK + 1) * CHUNK
    spare = jnp.max(jnp.where(taken != jnp.minimum(topk, admissible).astype(F32), 1.0, 0.0)) > 0.0

    @pl.when(spare)
    def _():
        lower = (lax.broadcasted_iota(jnp.int32, (kb, kb), 1)
                 < lax.broadcasted_iota(jnp.int32, (kb, kb), 0)).astype(BF16)

        def ranked_block(j, tie_carry):
            ks = blk_slice(j)
            blk = key_sc[ks, :]
            tie = blk == thr
            tie_f = jnp.where(tie, 1.0, 0.0)
            rank = jnp.dot(lower, tie_f.astype(BF16), preferred_element_type=F32) + tie_carry
            sel = ((blk > thr) | (tie & (rank < need))) & (blk != INT_MIN)
            msk_sc[ks, :] = jnp.where(sel, 0.0, MASK_NEG)
            return tie_carry + jnp.sum(tie_f, axis=0, keepdims=True)

        lax.fori_loop(0, nkb, ranked_block, jnp.zeros((1, tq), F32))

    oT_sc[...] = jnp.zeros_like(oT_sc)
    m_sc[...] = jnp.full((N_HEADS, tq), M_INIT, F32)

    def logits_stage(j, slot):
        ks = blk_slice(j)
        d = jnp.minimum(i - j, 2)
        msk = msk_sc[ks, :]
        for h in range(N_HEADS):
            hs = slice(h * HEAD_DIM, (h + 1) * HEAD_DIM)
            lg = jnp.dot(k_ref[0, ks, hs], qT_ref[0, hs, :],
                         preferred_element_type=F32)
            lg = lg + bias_ref[h, d] + msk
            lg_sc[slot, h] = lg
            mb_sc[slot, h:h + 1, :] = jnp.max(lg, axis=0, keepdims=True)

    def softmax_stage(j, slot):
        m_old = m_sc[...]
        m_new = jnp.maximum(m_old, mb_sc[slot])
        alpha = jnp.exp(m_old - m_new)
        for h in range(N_HEADS):
            vs = slice(h * V_ROWS, (h + 1) * V_ROWS)
            p = jnp.exp(lg_sc[slot, h] - m_new[h:h + 1, :])
            oT_sc[vs, :] = alpha[h:h + 1, :] * oT_sc[vs, :] + jnp.dot(
                vT_ref[0, j, vs, :], p.astype(BF16), preferred_element_type=F32)
        m_sc[...] = m_new

    logits_stage(0, 0)

    def attn_block(j, _):
        slot = j & 1
        softmax_stage(j - 1, 1 - slot)
        logits_stage(j, slot)
        return 0

    lax.fori_loop(1, nkb, attn_block, 0)
    softmax_stage(nkb - 1, (nkb - 1) & 1)
    for h in range(N_HEADS):
        acc = oT_sc[h * V_ROWS:h * V_ROWS + HEAD_DIM, :]
        l = oT_sc[h * V_ROWS + HEAD_DIM:h * V_ROWS + HEAD_DIM + 1, :]
        on_sc[h * HEAD_DIM:(h + 1) * HEAD_DIM, :] = acc / l
    o_ref[0] = on_sc[...].T.astype(BF16)


def _sparse(qiT, kw, wiT, qT, k, vT, bias_tiles, tq, topk):
    b, s, _ = k.shape
    W = W_ATT
    kern = functools.partial(_sparse_kernel, tq=tq, topk=topk)
    return pl.pallas_call(
        kern,
        grid=(b, s // tq),
        in_specs=[
            pl.BlockSpec((1, W, tq), lambda bi, i: (bi, 0, i)),
            pl.BlockSpec((1, s, 128), lambda bi, i: (bi, 0, 0)),
            pl.BlockSpec((1, N_IDX_HEADS, tq), lambda bi, i: (bi, 0, i)),
            pl.BlockSpec((1, W, tq), lambda bi, i: (bi, 0, i)),
            pl.BlockSpec((1, s, W), lambda bi, i: (bi, 0, 0)),
            pl.BlockSpec((1, s // tq, N_HEADS * V_ROWS, tq), lambda bi, i: (bi, 0, 0, 0)),
            pl.BlockSpec((N_HEADS, 3, tq, tq), lambda bi, i: (0, 0, 0, 0),
                         pipeline_mode=pl.Buffered(1)),
        ],
        out_specs=pl.BlockSpec((1, tq, W), lambda bi, i: (bi, i, 0)),
        out_shape=jax.ShapeDtypeStruct((b, s, W), BF16),
        scratch_shapes=[
            pltpu.VMEM((s, tq), jnp.int32),
            pltpu.VMEM((s, tq), jnp.int16),
            pltpu.VMEM((s, tq), jnp.int16),
            pltpu.VMEM((s, tq), jnp.int16),
            pltpu.VMEM((s, tq), F32),
            pltpu.VMEM((2, N_HEADS, tq, tq), F32),
            pltpu.VMEM((2, N_HEADS, tq), F32),
            pltpu.VMEM((N_HEADS, tq), F32),
            pltpu.VMEM((N_HEADS * V_ROWS, tq), F32),
            pltpu.VMEM((W, tq), F32),
        ],
        compiler_params=pltpu.CompilerParams(
            dimension_semantics=("arbitrary", "arbitrary"), vmem_limit_bytes=VMEM_LIMIT_BYTES),
        name="sparse_attention",
    )(qiT, kw, wiT, qT, k, vT, bias_tiles)


def _sb_kernel(qT_ref, k_ref, vT_ref, o_ref, z_sc, hi_sc, lo_sc, a_sc, accT_sc, *, tq, kb):
    i = pl.program_id(1)
    t0 = i * tq
    nb = tq // kb
    row = lax.broadcasted_iota(jnp.int32, (kb, tq), 0)
    col = lax.broadcasted_iota(jnp.int32, (kb, tq), 1)
    upper = (lax.broadcasted_iota(jnp.int32, (kb, kb), 1)
             >= lax.broadcasted_iota(jnp.int32, (kb, kb), 0)).astype(BF16)
    accT_sc[...] = jnp.zeros_like(accT_sc)

    def causal(j):
        return (j * kb + row) < (t0 + col)

    def keep_stage(j, slot, valid, l0=0):
        ks = pl.ds(pl.multiple_of(j * kb, kb), kb)
        sums = []
        for h in range(N_HEADS):
            hs = slice(h * HEAD_DIM, (h + 1) * HEAD_DIM)
            z = jnp.dot(k_ref[0, ks, hs], qT_ref[0, hs, l0:],
                        preferred_element_type=F32)
            lk = -(jnp.maximum(z, 0.0) + jnp.log(1.0 + jnp.exp(-jnp.abs(z))))
            if valid is not None:
                lk = jnp.where(valid[:, l0:], lk, 0.0)
            hi = lk.astype(BF16)
            z_sc[slot, h, :, l0:] = z
            hi_sc[slot, h, :, l0:] = hi
            lo_sc[slot, h, :, l0:] = (lk - hi.astype(F32)).astype(BF16)
            sums.append(jnp.sum(lk, axis=0, keepdims=True))
        sums = jnp.concatenate(sums, axis=0)
        if l0:
            sums = jnp.concatenate([jnp.zeros((N_HEADS, l0), F32), sums], axis=1)
        return sums

    def weight_stage(slot, carry, valid, l0=0):
        for h in range(N_HEADS):
            suf = (jnp.dot(upper, hi_sc[slot, h, :, l0:], preferred_element_type=F32)
                   + jnp.dot(upper, lo_sc[slot, h, :, l0:], preferred_element_type=F32))
            a = jnp.exp(z_sc[slot, h, :, l0:] + suf + carry[h:h + 1, l0:])
            if valid is not None:
                a = jnp.where(valid[:, l0:], a, 0.0)
            a_sc[h, :, l0:] = a.astype(BF16)

    def value_stage(j, when, l0=0):
        @pl.when(when)
        def _():
            for h in range(N_HEADS):
                hs = slice(h * HEAD_DIM, (h + 1) * HEAD_DIM)
                accT_sc[hs, l0:] += jnp.dot(vT_ref[0, j, hs, :], a_sc[h, :, l0:],
                                            preferred_element_type=F32)

    j_top = (i + 1) * nb - 1
    first_lane = [(nb - 1 - r) * kb for r in range(nb)]
    carries = [jnp.zeros((N_HEADS, tq), F32)]
    carries.append(keep_stage(j_top, 0, causal(j_top), first_lane[0]))
    for r in range(1, nb):
        weight_stage((r - 1) % 2, carries[r - 1], causal(j_top - r + 1), first_lane[r - 1])
        carries.append(carries[r] + keep_stage(j_top - r, r % 2, causal(j_top - r), first_lane[r]))
        value_stage(j_top - r + 1, i >= 0, first_lane[r - 1])
    last_slot = (nb - 1) % 2
    j_last = j_top - nb + 1
    c_pend, c_next = carries[nb - 1], carries[nb]

    def only_diagonal():
        weight_stage(last_slot, c_pend, causal(j_last))
        value_stage(j_last, i == 0)

    def more_blocks():
        weight_stage(last_slot, c_pend, causal(j_last))
        c2 = c_next + keep_stage(j_last - 1, 1 - last_slot, None)
        value_stage(j_last, i >= 1)

        def cond(st):
            j, _, _, c_nxt = st
            return jnp.logical_and(j >= 0, jnp.max(c_nxt) > SB_EXIT)

        def body(st):
            j, slot, c_pnd, c_nxt = st
            weight_stage(slot, c_pnd, None)
            sums = keep_stage(j, 1 - slot, None)
            value_stage(j + 1, j >= 0)
            return j - 1, 1 - slot, c_nxt, c_nxt + sums

        j, slot, c_pnd, _ = lax.while_loop(
            cond, body, (j_last - 2, jnp.int32(1 - last_slot), c_next, c2))
        weight_stage(slot, c_pnd, None)
        value_stage(j + 1, i >= 1)

    lax.cond(i >= 1, more_blocks, only_diagonal)
    o_ref[0] = accT_sc[...].T.astype(BF16)


def _stick_breaking(qT, k, vT, tq, kb):
    b, s, _ = k.shape
    W = W_ATT
    kern = functools.partial(_sb_kernel, tq=tq, kb=kb)
    return pl.pallas_call(
        kern,
        grid=(b, s // tq),
        in_specs=[
            pl.BlockSpec((1, W, tq), lambda bi, i: (bi, 0, i)),
            pl.BlockSpec((1, s, W), lambda bi, i: (bi, 0, 0)),
            pl.BlockSpec((1, s // kb, W, kb), lambda bi, i: (bi, 0, 0, 0)),
        ],
        out_specs=pl.BlockSpec((1, tq, W), lambda bi, i: (bi, i, 0)),
        out_shape=jax.ShapeDtypeStruct((b, s, W), BF16),
        scratch_shapes=[
            pltpu.VMEM((2, N_HEADS, kb, tq), F32),
            pltpu.VMEM((2, N_HEADS, kb, tq), BF16),
            pltpu.VMEM((2, N_HEADS, kb, tq), BF16),
            pltpu.VMEM((N_HEADS, kb, tq), BF16),
            pltpu.VMEM((W, tq), F32),
        ],
        compiler_params=pltpu.CompilerParams(
            dimension_semantics=("arbitrary", "arbitrary"), vmem_limit_bytes=VMEM_LIMIT_BYTES),
        name="stick_breaking_attention",
    )(qT, k, vT)


def _merge_ffn_kernel(x_ref, oa_ref, ob_ref, gates_ref, wa_ref, wb_ref, wo_ref, gmix_ref,
                      gpre_ref, wgu_ref, wd_ref, gpost_ref, xo_ref, x1_sc, acc_sc, *, dff, fc):
    d = x_ref.shape[1]
    pa = jnp.dot(oa_ref[...], wa_ref[...], preferred_element_type=F32)
    pb = jnp.dot(ob_ref[...], wb_ref[...], preferred_element_type=F32)
    m = jax.nn.sigmoid(gates_ref[:, 0:d]) * pa + jax.nn.sigmoid(gates_ref[:, d:2 * d]) * pb
    y = jnp.dot(m.astype(BF16), wo_ref[...], preferred_element_type=F32)
    x1_sc[...] = x_ref[...] + _rms(y, gmix_ref[...])
    h = _rms(x1_sc[...], gpre_ref[...]).astype(BF16)
    for c in range(0, dff, fc):
        gate = jnp.dot(h, wgu_ref[:, c:c + fc], preferred_element_type=F32)
        up = jnp.dot(h, wgu_ref[:, dff + c:dff + c + fc], preferred_element_type=F32)
        act = (jax.nn.silu(gate) * up).astype(BF16)
        part = jnp.dot(act, wd_ref[c:c + fc, :], preferred_element_type=F32)
        if c == 0:
            acc_sc[...] = part
        else:
            acc_sc[...] += part
    xo_ref[...] = x1_sc[...] + _rms(acc_sc[...], gpost_ref[...])


def _merge_ffn(x2, oa, ob, gates, wa, wb, wo, gmix, gpre, wgu, wd, gpost, tm, fc):
    n, d = x2.shape
    W = W_ATT
    dff = wd.shape[0]
    row = lambda i: (i, 0)
    const = lambda i: (0, 0)
    resident = functools.partial(pl.BlockSpec, index_map=const, pipeline_mode=pl.Buffered(1))
    kern = functools.partial(_merge_ffn_kernel, dff=dff, fc=fc)
    return pl.pallas_call(
        kern,
        grid=(n // tm,),
        in_specs=[
            pl.BlockSpec((tm, d), row),
            pl.BlockSpec((tm, W), row),
            pl.BlockSpec((tm, W), row),
            pl.BlockSpec((tm, 2 * d), row),
            resident((W, d)),
            resident((W, d)),
            resident((d, d)),
            pl.BlockSpec((1, d), const),
            pl.BlockSpec((1, d), const),
            resident((d, 2 * dff)),
            resident((dff, d)),
            pl.BlockSpec((1, d), const),
        ],
        out_specs=pl.BlockSpec((tm, d), row),
        out_shape=jax.ShapeDtypeStruct((n, d), F32),
        scratch_shapes=[pltpu.VMEM((tm, d), F32), pltpu.VMEM((tm, d), F32)],
        compiler_params=pltpu.CompilerParams(
            dimension_semantics=("arbitrary",), vmem_limit_bytes=VMEM_LIMIT_BYTES),
        name="merge_swiglu",
    )(x2, oa, ob, gates, wa, wb, wo, gmix, gpre, wgu, wd, gpost)


def _t5_bucket(rel):
    nb = N_BUCKETS // 2
    max_exact = nb // 2
    ret = (rel > 0).astype(jnp.int32) * nb
    n = jnp.abs(rel)
    nf = jnp.maximum(n, 1).astype(jnp.float32)
    large = max_exact + (jnp.log(nf / max_exact) / math.log(MAX_DISTANCE / max_exact)
                         * (nb - max_exact)).astype(jnp.int32)
    large = jnp.minimum(large, nb - 1)
    return ret + jnp.where(n < max_exact, n, large)


def _bias_tables(rel_bias, tq):
    s_l = jnp.arange(tq, dtype=jnp.int32)[:, None]
    t_l = jnp.arange(tq, dtype=jnp.int32)[None, :]
    rel = jnp.stack([s_l - t_l - d * tq for d in range(3)])
    onehot = (_t5_bucket(rel)[..., None] == jnp.arange(N_BUCKETS)).astype(F32)
    return jnp.einsum("dstb,bh->hdst", onehot, rel_bias.astype(F32), precision=lax.Precision.HIGHEST)


def _pack_w_in(w):
    d = w.shape[0]
    head = 3 * W_ATT + N_IDX_HEADS * IDX_DIM
    small = IDX_DIM + N_IDX_HEADS
    pad = jnp.zeros((d, 128 - small), w.dtype)
    return jnp.concatenate([w[:, :head], w[:, head + small:], w[:, head:head + small], pad],
                           axis=1).astype(BF16)


def kernel(x, w_in, w_branch_sparse, w_branch_sb, w_out, w_gate_up, w_down,
           g_pre_mix, g_post_mix, g_pre_ffn, g_post_ffn, rel_bias):
    b, s, d = x.shape
    depth = w_in.shape[0]
    n = b * s
    W = W_ATT
    tm = min(512, s)
    tq = min(256, s)
    kb_sb = min(128, tq)
    topk = min(INDEX_TOPK, s // 4)
    dff = w_down.shape[1]
    fc = 256
    assert s % tm == 0 and tm % tq == 0 and dff % fc == 0 and tq % CHUNK == 0

    bias_tiles = _bias_tables(rel_bias, tq)
    x2 = x.reshape(n, d)
    for l in range(depth):
        ka, kb, gates, kw, qaT, qiT, qbT, vTa, vTb, wiT = _proj(
            x2, g_pre_mix[l][None], _pack_w_in(w_in[l]), tm, tq, kb_sb, s)
        o_a = _sparse(qiT, kw.reshape(b, s, 128), wiT, qaT, ka.reshape(b, s, W), vTa,
                      bias_tiles, tq, topk)
        o_b = _stick_breaking(qbT, kb.reshape(b, s, W), vTb, tq, kb_sb)
        x2 = _merge_ffn(x2, o_a.reshape(n, W), o_b.reshape(n, W), gates,
                        w_branch_sparse[l].astype(BF16), w_branch_sb[l].astype(BF16),
                        w_out[l].astype(BF16), g_post_mix[l][None], g_pre_ffn[l][None],
                        w_gate_up[l].astype(BF16), w_down[l].astype(BF16), g_post_ffn[l][None], tm, fc)
    return x2.reshape(b, s, d)
```

```python
import functools
import math

import jax
import jax.numpy as jnp
from jax import lax
from jax.experimental import pallas as pl
from jax.experimental.pallas import tpu as pltpu

F32 = jnp.float32
BF16 = jnp.bfloat16

CHUNK = 64
HEAD_DIM = 64
N_HEADS = 8
W_ATT = N_HEADS * HEAD_DIM
N_IDX_HEADS = 8
IDX_DIM = 64
INDEX_TOPK = 256
N_BUCKETS = 32
MAX_DISTANCE = 128
RMS_EPS = 1e-6

INT_MIN = -(2 ** 31)
MASK_NEG = -1e30
M_INIT = -1e29
V_ROWS = HEAD_DIM + 16
LOW_HALF_CHECK = 10
SB_EXIT = -88.0

VMEM_LIMIT_BYTES = 58 * 1024 * 1024


def _rms(x, g):
    ms = jnp.mean(x * x, axis=-1, keepdims=True)
    return x * lax.rsqrt(ms + RMS_EPS) * g


def _proj_kernel(x_ref, g_ref, w_ref, ka_ref, kb_ref, gates_ref, kw_ref,
                 qaT_ref, qiT_ref, qbT_ref, vTa_ref, vTb_ref, wiT_ref, *, tq, kb_sb):
    h = _rms(x_ref[...], g_ref[...]).astype(BF16)
    tm, d = x_ref.shape
    W = W_ATT

    def mm(c0, c1):
        return jnp.dot(h, w_ref[:, c0:c1], preferred_element_type=F32)

    qaT_ref[0] = (mm(0, W) * 0.125).T.astype(BF16)
    ka_ref[...] = mm(W, 2 * W).astype(BF16)
    qiT_ref[0] = (mm(3 * W, 4 * W) * 0.125).T.astype(BF16)
    qbT_ref[0] = (mm(4 * W, 5 * W) * 0.125).T.astype(BF16)
    kb_ref[...] = mm(5 * W, 6 * W).astype(BF16)
    for c in range(0, 2 * d, W):
        gates_ref[:, c:c + W] = mm(7 * W + c, 7 * W + c + W)
    kw = mm(7 * W + 2 * d, 7 * W + 2 * d + 128)
    kw_ref[...] = kw
    wiT_ref[0] = kw.T[IDX_DIM:IDX_DIM + N_IDX_HEADS, :]

    vaT = mm(2 * W, 3 * W).T.astype(BF16)
    vbT = mm(6 * W, 7 * W).T.astype(BF16)
    tail = (lax.broadcasted_iota(jnp.int32, (V_ROWS - HEAD_DIM, tq), 0) == 0).astype(BF16)
    for blk in range(tm // kb_sb):
        vTb_ref[0, blk] = vbT[:, blk * kb_sb:(blk + 1) * kb_sb]
    for blk in range(tm // tq):
        cols = slice(blk * tq, (blk + 1) * tq)
        for hd in range(N_HEADS):
            vTa_ref[0, blk, hd * V_ROWS:hd * V_ROWS + HEAD_DIM, :] = vaT[hd * HEAD_DIM:(hd + 1) * HEAD_DIM, cols]
            vTa_ref[0, blk, hd * V_ROWS + HEAD_DIM:(hd + 1) * V_ROWS, :] = tail


def _proj(x2, g, w_packed, tm, tq, kb_sb, s):
    n, d = x2.shape
    wcols = w_packed.shape[1]
    W = W_ATT
    tiles = s // tm
    nblk = tm // tq
    row = lambda i: (i, 0)
    seq = lambda i: (i // tiles, i % tiles, 0, 0)
    colT = lambda i: (i // tiles, 0, i % tiles)
    return pl.pallas_call(
        functools.partial(_proj_kernel, tq=tq, kb_sb=kb_sb),
        grid=(n // tm,),
        in_specs=[
            pl.BlockSpec((tm, d), row),
            pl.BlockSpec((1, d), lambda i: (0, 0)),
            pl.BlockSpec((d, wcols), lambda i: (0, 0), pipeline_mode=pl.Buffered(1)),
        ],
        out_specs=[
            pl.BlockSpec((tm, W), row),
            pl.BlockSpec((tm, W), row),
            pl.BlockSpec((tm, 2 * d), row),
            pl.BlockSpec((tm, 128), row),
            pl.BlockSpec((1, W, tm), colT),
            pl.BlockSpec((1, W, tm), colT),
            pl.BlockSpec((1, W, tm), colT),
            pl.BlockSpec((1, nblk, N_HEADS * V_ROWS, tq), seq),
            pl.BlockSpec((1, tm // kb_sb, W, kb_sb), seq),
            pl.BlockSpec((1, N_IDX_HEADS, tm), colT),
        ],
        out_shape=[
            jax.ShapeDtypeStruct((n, W), BF16),
            jax.ShapeDtypeStruct((n, W), BF16),
            jax.ShapeDtypeStruct((n, 2 * d), F32),
            jax.ShapeDtypeStruct((n, 128), F32),
            jax.ShapeDtypeStruct((n // s, W, s), BF16),
            jax.ShapeDtypeStruct((n // s, W, s), BF16),
            jax.ShapeDtypeStruct((n // s, W, s), BF16),
            jax.ShapeDtypeStruct((n // s, s // tq, N_HEADS * V_ROWS, tq), BF16),
            jax.ShapeDtypeStruct((n // s, s // kb_sb, W, kb_sb), BF16),
            jax.ShapeDtypeStruct((n // s, N_IDX_HEADS, s), F32),
        ],
        compiler_params=pltpu.CompilerParams(
            dimension_semantics=("arbitrary",), vmem_limit_bytes=VMEM_LIMIT_BYTES),
        name="pre_mix_proj",
    )(x2, g, w_packed)


def _sparse_kernel(qiT_ref, kw_ref, wiT_ref, qT_ref, k_ref, vT_ref, bias_ref, o_ref,
                   key_sc, hi_sc, lo_sc, lo2_sc, msk_sc, lg_sc, mb_sc, m_sc, oT_sc, on_sc, *, tq, topk):
    kb = tq
    i = pl.program_id(1)
    t0 = i * tq
    nkb = i + 1
    row = lax.broadcasted_iota(jnp.int32, (kb, tq), 0)
    col = lax.broadcasted_iota(jnp.int32, (kb, tq), 1)
    w = wiT_ref[0] * (N_IDX_HEADS ** -0.5)

    def blk_slice(j):
        return pl.ds(pl.multiple_of(j * kb, kb), kb)

    def score_block(j):
        ks = blk_slice(j)
        kib = kw_ref[0, ks, 0:IDX_DIM].astype(BF16)
        acc = jnp.zeros((kb, tq), F32)
        for h in range(N_IDX_HEADS):
            d = jnp.dot(kib, qiT_ref[0, h * IDX_DIM:(h + 1) * IDX_DIM, :],
                        preferred_element_type=F32)
            acc = acc + jnp.maximum(d, 0.0) * w[h:h + 1, :]
        adm = ((j * kb + row) // CHUNK) <= ((t0 + col) // CHUNK)
        bits = pltpu.bitcast(acc, jnp.int32)
        key = bits ^ ((bits >> 31) & 0x7FFFFFFF)
        key = jnp.where(adm, key, INT_MIN)
        key_sc[ks, :] = key
        hi_sc[ks, :] = (key >> 16).astype(jnp.int16)
        lo_sc[ks, :] = ((key & 0xFFFF) - 32768).astype(jnp.int16)

    def score_pair(jj, _):
        score_block(2 * jj)
        score_block(jnp.minimum(2 * jj + 1, nkb - 1))
        return 0

    lax.fori_loop(0, (nkb + 1) // 2, score_pair, 0)

    def count16(ref, pred):
        one = jnp.ones((), BF16)
        zero = jnp.zeros((), BF16)

        def body(j, accs):
            hit = jnp.where(pred(ref[blk_slice(j), :]), one, zero).reshape(kb // 16, 16, tq)
            accs = list(accs)
            for r in range(kb // 16):
                accs[r % 4] = accs[r % 4] + hit[r]
            return tuple(accs)

        accs = lax.fori_loop(0, nkb, body, (jnp.zeros((16, tq), BF16),) * 4)
        tot = (accs[0].astype(F32) + accs[1].astype(F32)) + (accs[2].astype(F32) + accs[3].astype(F32))
        return jnp.sum(tot, axis=0, keepdims=True)

    def search16(ref, kth, check_after=None):
        def bit_step(b, st):
            thr, taken_at_thr = st
            cand = thr + lax.shift_left(jnp.int32(1), 15 - b)
            cand16 = cand.astype(jnp.int16)
            c = count16(ref, lambda blk: blk >= cand16)
            ok = c >= kth
            return jnp.where(ok, cand, thr), jnp.where(ok, c, taken_at_thr)

        st = (jnp.full((1, tq), -32768, jnp.int32), jnp.full((1, tq), -1.0, F32))
        if check_after is None:
            return lax.fori_loop(0, 16, bit_step, st)[0]
        st = lax.fori_loop(0, check_after, bit_step, st)
        unresolved = jnp.max(jnp.where(st[1] != kth, 1.0, 0.0)) > 0.0
        st = lax.cond(unresolved, lambda s: lax.fori_loop(check_after, 16, bit_step, s), lambda s: s, st)
        return st[0]

    thr_hi = search16(hi_sc, float(topk))
    thr_hi16 = thr_hi.astype(jnp.int16)

    def low_half_block(j, accs):
        ks = blk_slice(j)
        hi = hi_sc[ks, :]
        lo2_sc[ks, :] = jnp.where(hi == thr_hi16, lo_sc[ks, :], jnp.int16(-32768))
        hit = jnp.where(hi > thr_hi16, jnp.ones((), BF16), jnp.zeros((), BF16)).reshape(kb // 16, 16, tq)
        accs = list(accs)
        for r in range(kb // 16):
            accs[r % 4] = accs[r % 4] + hit[r]
        return tuple(accs)

    accs = lax.fori_loop(0, nkb, low_half_block, (jnp.zeros((16, tq), BF16),) * 4)
    above_hi = jnp.sum((accs[0].astype(F32) + accs[1].astype(F32))
                       + (accs[2].astype(F32) + accs[3].astype(F32)), axis=0, keepdims=True)
    thr_lo = search16(lo2_sc, topk - above_hi, check_after=LOW_HALF_CHECK)
    thr_lo16 = thr_lo.astype(jnp.int16)
    need = topk - (above_hi + count16(lo2_sc, lambda blk: blk > thr_lo16))
    thr = thr_hi * 65536 + (thr_lo + 32768)

    floor = jnp.maximum(thr, INT_MIN + 1)

    def mask_block(j, cnt):
        ks = blk_slice(j)
        sel = key_sc[ks, :] >= floor
        msk_sc[ks, :] = jnp.where(sel, 0.0, MASK_NEG)
        return cnt + jnp.sum(jnp.where(sel, 1.0, 0.0).reshape(kb // 8, 8, tq), axis=0)

    taken = jnp.sum(lax.fori_loop(0, nkb, mask_block, jnp.zeros((8, tq), F32)), axis=0, keepdims=True)
    admissible = ((t0 + col[0:1, :]) // CHUNK + 1) * CHUNK
    spare = jnp.max(jnp.where(taken != jnp.minimum(topk, admissible).astype(F32), 1.0, 0.0)) > 0.0

    @pl.when(spare)
    def _():
        lower = (lax.broadcasted_iota(jnp.int32, (kb, kb), 1)
                 < lax.broadcasted_iota(jnp.int32, (kb, kb), 0)).astype(BF16)

        def ranked_block(j, tie_carry):
            ks = blk_slice(j)
            blk = key_sc[ks, :]
            tie = blk == thr
            tie_f = jnp.where(tie, 1.0, 0.0)
            rank = jnp.dot(lower, tie_f.astype(BF16), preferred_element_type=F32) + tie_carry
            sel = ((blk > thr) | (tie & (rank < need))) & (blk != INT_MIN)
            msk_sc[ks, :] = jnp.where(sel, 0.0, MASK_NEG)
            return tie_carry + jnp.sum(tie_f, axis=0, keepdims=True)

        lax.fori_loop(0, nkb, ranked_block, jnp.zeros((1, tq), F32))

    oT_sc[...] = jnp.zeros_like(oT_sc)
    m_sc[...] = jnp.full((N_HEADS, tq), M_INIT, F32)

    def logits_stage(j, slot):
        ks = blk_slice(j)
        d = jnp.minimum(i - j, 2)
        msk = msk_sc[ks, :]
        for h in range(N_HEADS):
            hs = slice(h * HEAD_DIM, (h + 1) * HEAD_DIM)
            lg = jnp.dot(k_ref[0, ks, hs], qT_ref[0, hs, :],
                         preferred_element_type=F32)
            lg = lg + bias_ref[h, d] + msk
            lg_sc[slot, h] = lg
            mb_sc[slot, h:h + 1, :] = jnp.max(lg, axis=0, keepdims=True)

    def softmax_stage(j, slot):
        m_old = m_sc[...]
        m_new = jnp.maximum(m_old, mb_sc[slot])
        alpha = jnp.exp(m_old - m_new)
        for h in range(N_HEADS):
            vs = slice(h * V_ROWS, (h + 1) * V_ROWS)
            p = jnp.exp(lg_sc[slot, h] - m_new[h:h + 1, :])
            oT_sc[vs, :] = alpha[h:h + 1, :] * oT_sc[vs, :] + jnp.dot(
                vT_ref[0, j, vs, :], p.astype(BF16), preferred_element_type=F32)
        m_sc[...] = m_new

    logits_stage(0, 0)

    def attn_block(j, _):
        slot = j & 1
        softmax_stage(j - 1, 1 - slot)
        logits_stage(j, slot)
        return 0

    lax.fori_loop(1, nkb, attn_block, 0)
    softmax_stage(nkb - 1, (nkb - 1) & 1)
    for h in range(N_HEADS):
        acc = oT_sc[h * V_ROWS:h * V_ROWS + HEAD_DIM, :]
        l = oT_sc[h * V_ROWS + HEAD_DIM:h * V_ROWS + HEAD_DIM + 1, :]
        on_sc[h * HEAD_DIM:(h + 1) * HEAD_DIM, :] = acc / l
    o_ref[0] = on_sc[...].T.astype(BF16)


def _sparse(qiT, kw, wiT, qT, k, vT, bias_tiles, tq, topk):
    b, s, _ = k.shape
    W = W_ATT
    kern = functools.partial(_sparse_kernel, tq=tq, topk=topk)
    return pl.pallas_call(
        kern,
        grid=(b, s // tq),
        in_specs=[
            pl.BlockSpec((1, W, tq), lambda bi, i: (bi, 0, i)),
            pl.BlockSpec((1, s, 128), lambda bi, i: (bi, 0, 0)),
            pl.BlockSpec((1, N_IDX_HEADS, tq), lambda bi, i: (bi, 0, i)),
            pl.BlockSpec((1, W, tq), lambda bi, i: (bi, 0, i)),
            pl.BlockSpec((1, s, W), lambda bi, i: (bi, 0, 0)),
            pl.BlockSpec((1, s // tq, N_HEADS * V_ROWS, tq), lambda bi, i: (bi, 0, 0, 0)),
            pl.BlockSpec((N_HEADS, 3, tq, tq), lambda bi, i: (0, 0, 0, 0),
                         pipeline_mode=pl.Buffered(1)),
        ],
        out_specs=pl.BlockSpec((1, tq, W), lambda bi, i: (bi, i, 0)),
        out_shape=jax.ShapeDtypeStruct((b, s, W), BF16),
        scratch_shapes=[
            pltpu.VMEM((s, tq), jnp.int32),
            pltpu.VMEM((s, tq), jnp.int16),
            pltpu.VMEM((s, tq), jnp.int16),
            pltpu.VMEM((s, tq), jnp.int16),
            pltpu.VMEM((s, tq), F32),
            pltpu.VMEM((2, N_HEADS, tq, tq), F32),
            pltpu.VMEM((2, N_HEADS, tq), F32),
            pltpu.VMEM((N_HEADS, tq), F32),
            pltpu.VMEM((N_HEADS * V_ROWS, tq), F32),
            pltpu.VMEM((W, tq), F32),
        ],
        compiler_params=pltpu.CompilerParams(
            dimension_semantics=("arbitrary", "arbitrary"), vmem_limit_bytes=VMEM_LIMIT_BYTES),
        name="sparse_attention",
    )(qiT, kw, wiT, qT, k, vT, bias_tiles)


def _sb_kernel(qT_ref, k_ref, vT_ref, o_ref, z_sc, hi_sc, lo_sc, a_sc, accT_sc, *, tq, kb):
    i = pl.program_id(1)
    t0 = i * tq
    nb = tq // kb
    row = lax.broadcasted_iota(jnp.int32, (kb, tq), 0)
    col = lax.broadcasted_iota(jnp.int32, (kb, tq), 1)
    upper = (lax.broadcasted_iota(jnp.int32, (kb, kb), 1)
             >= lax.broadcasted_iota(jnp.int32, (kb, kb), 0)).astype(BF16)
    accT_sc[...] = jnp.zeros_like(accT_sc)

    def causal(j):
        return (j * kb + row) < (t0 + col)

    def keep_stage(j, slot, valid, l0=0):
        ks = pl.ds(pl.multiple_of(j * kb, kb), kb)
        sums = []
        for h in range(N_HEADS):
            hs = slice(h * HEAD_DIM, (h + 1) * HEAD_DIM)
            z = jnp.dot(k_ref[0, ks, hs], qT_ref[0, hs, l0:],
                        preferred_element_type=F32)
            lk = -(jnp.maximum(z, 0.0) + jnp.log(1.0 + jnp.exp(-jnp.abs(z))))
            if valid is not None:
                lk = jnp.where(valid[:, l0:], lk, 0.0)
            hi = lk.astype(BF16)
            z_sc[slot, h, :, l0:] = z
            hi_sc[slot, h, :, l0:] = hi
            lo_sc[slot, h, :, l0:] = (lk - hi.astype(F32)).astype(BF16)
            sums.append(jnp.sum(lk, axis=0, keepdims=True))
        sums = jnp.concatenate(sums, axis=0)
        if l0:
            sums = jnp.concatenate([jnp.zeros((N_HEADS, l0), F32), sums], axis=1)
        return sums

    def weight_stage(slot, carry, valid, l0=0):
        for h in range(N_HEADS):
            suf = (jnp.dot(upper, hi_sc[slot, h, :, l0:], preferred_element_type=F32)
                   + jnp.dot(upper, lo_sc[slot, h, :, l0:], preferred_element_type=F32))
            a = jnp.exp(z_sc[slot, h, :, l0:] + suf + carry[h:h + 1, l0:])
            if valid is not None:
                a = jnp.where(valid[:, l0:], a, 0.0)
            a_sc[h, :, l0:] = a.astype(BF16)

    def value_stage(j, when, l0=0):
        @pl.when(when)
        def _():
            for h in range(N_HEADS):
                hs = slice(h * HEAD_DIM, (h + 1) * HEAD_DIM)
                accT_sc[hs, l0:] += jnp.dot(vT_ref[0, j, hs, :], a_sc[h, :, l0:],
                                            preferred_element_type=F32)

    j_top = (i + 1) * nb - 1
    first_lane = [(nb - 1 - r) * kb for r in range(nb)]
    carries = [jnp.zeros((N_HEADS, tq), F32)]
    carries.append(keep_stage(j_top, 0, causal(j_top), first_lane[0]))
    for r in range(1, nb):
        weight_stage((r - 1) % 2, carries[r - 1], causal(j_top - r + 1), first_lane[r - 1])
        carries.append(carries[r] + keep_stage(j_top - r, r % 2, causal(j_top - r), first_lane[r]))
        value_stage(j_top - r + 1, i >= 0, first_lane[r - 1])
    last_slot = (nb - 1) % 2
    j_last = j_top - nb + 1
    c_pend, c_next = carries[nb - 1], carries[nb]

    def only_diagonal():
        weight_stage(last_slot, c_pend, causal(j_last))
        value_stage(j_last, i == 0)

    def more_blocks():
        weight_stage(last_slot, c_pend, causal(j_last))
        c2 = c_next + keep_stage(j_last - 1, 1 - last_slot, None)
        value_stage(j_last, i >= 1)

        def cond(st):
            j, _, _, c_nxt = st
            return jnp.logical_and(j >= 0, jnp.max(c_nxt) > SB_EXIT)

        def body(st):
            j, slot, c_pnd, c_nxt = st
            weight_stage(slot, c_pnd, None)
            sums = keep_stage(j, 1 - slot, None)
            value_stage(j + 1, j >= 0)
            return j - 1, 1 - slot, c_nxt, c_nxt + sums

        j, slot, c_pnd, _ = lax.while_loop(
            cond, body, (j_last - 2, jnp.int32(1 - last_slot), c_next, c2))
        weight_stage(slot, c_pnd, None)
        value_stage(j + 1, i >= 1)

    lax.cond(i >= 1, more_blocks, only_diagonal)
    o_ref[0] = accT_sc[...].T.astype(BF16)


def _stick_breaking(qT, k, vT, tq, kb):
    b, s, _ = k.shape
    W = W_ATT
    kern = functools.partial(_sb_kernel, tq=tq, kb=kb)
    return pl.pallas_call(
        kern,
        grid=(b, s // tq),
        in_specs=[
            pl.BlockSpec((1, W, tq), lambda bi, i: (bi, 0, i)),
            pl.BlockSpec((1, s, W), lambda bi, i: (bi, 0, 0)),
            pl.BlockSpec((1, s // kb, W, kb), lambda bi, i: (bi, 0, 0, 0)),
        ],
        out_specs=pl.BlockSpec((1, tq, W), lambda bi, i: (bi, i, 0)),
        out_shape=jax.ShapeDtypeStruct((b, s, W), BF16),
        scratch_shapes=[
            pltpu.VMEM((2, N_HEADS, kb, tq), F32),
            pltpu.VMEM((2, N_HEADS, kb, tq), BF16),
            pltpu.VMEM((2, N_HEADS, kb, tq), BF16),
            pltpu.VMEM((N_HEADS, kb, tq), BF16),
            pltpu.VMEM((W, tq), F32),
        ],
        compiler_params=pltpu.CompilerParams(
            dimension_semantics=("arbitrary", "arbitrary"), vmem_limit_bytes=VMEM_LIMIT_BYTES),
        name="stick_breaking_attention",
    )(qT, k, vT)


def _merge_ffn_kernel(x_ref, oa_ref, ob_ref, gates_ref, wa_ref, wb_ref, wo_ref, gmix_ref,
                      gpre_ref, wgu_ref, wd_ref, gpost_ref, xo_ref, x1_sc, acc_sc, *, dff, fc):
    d = x_ref.shape[1]
    pa = jnp.dot(oa_ref[...], wa_ref[...], preferred_element_type=F32)
    pb = jnp.dot(ob_ref[...], wb_ref[...], preferred_element_type=F32)
    m = jax.nn.sigmoid(gates_ref[:, 0:d]) * pa + jax.nn.sigmoid(gates_ref[:, d:2 * d]) * pb
    y = jnp.dot(m.astype(BF16), wo_ref[...], preferred_element_type=F32)
    x1_sc[...] = x_ref[...] + _rms(y, gmix_ref[...])
    h = _rms(x1_sc[...], gpre_ref[...]).astype(BF16)
    for c in range(0, dff, fc):
        gate = jnp.dot(h, wgu_ref[:, c:c + fc], preferred_element_type=F32)
        up = jnp.dot(h, wgu_ref[:, dff + c:dff + c + fc], preferred_element_type=F32)
        act = (jax.nn.silu(gate) * up).astype(BF16)
        part = jnp.dot(act, wd_ref[c:c + fc, :], preferred_element_type=F32)
        if c == 0:
            acc_sc[...] = part
        else:
            acc_sc[...] += part
    xo_ref[...] = x1_sc[...] + _rms(acc_sc[...], gpost_ref[...])


def _merge_ffn(x2, oa, ob, gates, wa, wb, wo, gmix, gpre, wgu, wd, gpost, tm, fc):
    n, d = x2.shape
    W = W_ATT
    dff = wd.shape[0]
    row = lambda i: (i, 0)
    const = lambda i: (0, 0)
    resident = functools.partial(pl.BlockSpec, index_map=const, pipeline_mode=pl.Buffered(1))
    kern = functools.partial(_merge_ffn_kernel, dff=dff, fc=fc)
    return pl.pallas_call(
        kern,
        grid=(n // tm,),
        in_specs=[
            pl.BlockSpec((tm, d), row),
            pl.BlockSpec((tm, W), row),
            pl.BlockSpec((tm, W), row),
            pl.BlockSpec((tm, 2 * d), row),
            resident((W, d)),
            resident((W, d)),
            resident((d, d)),
            pl.BlockSpec((1, d), const),
            pl.BlockSpec((1, d), const),
            resident((d, 2 * dff)),
            resident((dff, d)),
            pl.BlockSpec((1, d), const),
        ],
        out_specs=pl.BlockSpec((tm, d), row),
        out_shape=jax.ShapeDtypeStruct((n, d), F32),
        scratch_shapes=[pltpu.VMEM((tm, d), F32), pltpu.VMEM((tm, d), F32)],
        compiler_params=pltpu.CompilerParams(
            dimension_semantics=("arbitrary",), vmem_limit_bytes=VMEM_LIMIT_BYTES),
        name="merge_swiglu",
    )(x2, oa, ob, gates, wa, wb, wo, gmix, gpre, wgu, wd, gpost)


def _t5_bucket(rel):
    nb = N_BUCKETS // 2
    max_exact = nb // 2
    ret = (rel > 0).astype(jnp.int32) * nb
    n = jnp.abs(rel)
    nf = jnp.maximum(n, 1).astype(jnp.float32)
    large = max_exact + (jnp.log(nf / max_exact) / math.log(MAX_DISTANCE / max_exact)
                         * (nb - max_exact)).astype(jnp.int32)
    large = jnp.minimum(large, nb - 1)
    return ret + jnp.where(n < max_exact, n, large)


def _bias_tables(rel_bias, tq):
    s_l = jnp.arange(tq, dtype=jnp.int32)[:, None]
    t_l = jnp.arange(tq, dtype=jnp.int32)[None, :]
    rel = jnp.stack([s_l - t_l - d * tq for d in range(3)])
    onehot = (_t5_bucket(rel)[..., None] == jnp.arange(N_BUCKETS)).astype(F32)
    return jnp.einsum("dstb,bh->hdst", onehot, rel_bias.astype(F32), precision=lax.Precision.HIGHEST)


def _pack_w_in(w):
    d = w.shape[0]
    head = 3 * W_ATT + N_IDX_HEADS * IDX_DIM
    small = IDX_DIM + N_IDX_HEADS
    pad = jnp.zeros((d, 128 - small), w.dtype)
    return jnp.concatenate([w[:, :head], w[:, head + small:], w[:, head:head + small], pad],
                           axis=1).astype(BF16)


def kernel(x, w_in, w_branch_sparse, w_branch_sb, w_out, w_gate_up, w_down,
           g_pre_mix, g_post_mix, g_pre_ffn, g_post_ffn, rel_bias):
    b, s, d = x.shape
    depth = w_in.shape[0]
    n = b * s
    W = W_ATT
    tm = min(512, s)
    tq = min(256, s)
    kb_sb = min(128, tq)
    topk = min(INDEX_TOPK, s // 4)
    dff = w_down.shape[1]
    fc = 256
    assert s % tm == 0 and tm % tq == 0 and dff % fc == 0 and tq % CHUNK == 0

    bias_tiles = _bias_tables(rel_bias, tq)
    x2 = x.reshape(n, d)
    for l in range(depth):
        ka, kb, gates, kw, qaT, qiT, qbT, vTa, vTb, wiT = _proj(
            x2, g_pre_mix[l][None], _pack_w_in(w_in[l]), tm, tq, kb_sb, s)
        o_a = _sparse(qiT, kw.reshape(b, s, 128), wiT, qaT, ka.reshape(b, s, W), vTa,
                      bias_tiles, tq, topk)
        o_b = _stick_breaking(qbT, kb.reshape(b, s, W), vTb, tq, kb_sb)
        x2 = _merge_ffn(x2, o_a.reshape(n, W), o_b.reshape(n, W), gates,
                        w_branch_sparse[l].astype(BF16), w_branch_sb[l].astype(BF16),
                        w_out[l].astype(BF16), g_post_mix[l][None], g_pre_ffn[l][None],
                        w_gate_up[l].astype(BF16), w_down[l].astype(BF16), g_post_ffn[l][None], tm, fc)
    return x2.reshape(b, s, d)
```

```python
import functools
import math

import jax
import jax.numpy as jnp
from jax import lax
from jax.experimental import pallas as pl
from jax.experimental.pallas import tpu as pltpu

F32 = jnp.float32
BF16 = jnp.bfloat16

CHUNK = 64
HEAD_DIM = 64
N_HEADS = 8
W_ATT = N_HEADS * HEAD_DIM
N_IDX_HEADS = 8
IDX_DIM = 64
INDEX_TOPK = 256
N_BUCKETS = 32
MAX_DISTANCE = 128
RMS_EPS = 1e-6

INT_MIN = -(2 ** 31)
MASK_NEG = -1e30
M_INIT = -1e29
V_ROWS = HEAD_DIM + 16
LOW_HALF_CHECKS = (10, 13)
SB_EXIT = -88.0

VMEM_LIMIT_BYTES = 58 * 1024 * 1024


def _rms(x, g):
    ms = jnp.mean(x * x, axis=-1, keepdims=True)
    return x * lax.rsqrt(ms + RMS_EPS) * g


def _proj_kernel(x_ref, g_ref, w_ref, ka_ref, kb_ref, gates_ref, kw_ref,
                 qaT_ref, qiT_ref, qbT_ref, vTa_ref, vTb_ref, wiT_ref, *, tq, kb_sb):
    h = _rms(x_ref[...], g_ref[...]).astype(BF16)
    tm, d = x_ref.shape
    W = W_ATT

    def mm(c0, c1):
        return jnp.dot(h, w_ref[:, c0:c1], preferred_element_type=F32)

    qaT_ref[0] = (mm(0, W) * 0.125).T.astype(BF16)
    ka_ref[...] = mm(W, 2 * W).astype(BF16)
    qiT_ref[0] = (mm(3 * W, 4 * W) * 0.125).T.astype(BF16)
    qbT_ref[0] = (mm(4 * W, 5 * W) * 0.125).T.astype(BF16)
    kb_ref[...] = mm(5 * W, 6 * W).astype(BF16)
    for c in range(0, 2 * d, W):
        gates_ref[:, c:c + W] = mm(7 * W + c, 7 * W + c + W)
    kw = mm(7 * W + 2 * d, 7 * W + 2 * d + 128)
    kw_ref[...] = kw
    wiT_ref[0] = kw.T[IDX_DIM:IDX_DIM + N_IDX_HEADS, :]

    vaT = mm(2 * W, 3 * W).T.astype(BF16)
    vbT = mm(6 * W, 7 * W).T.astype(BF16)
    tail = (lax.broadcasted_iota(jnp.int32, (V_ROWS - HEAD_DIM, tq), 0) == 0).astype(BF16)
    for blk in range(tm // kb_sb):
        vTb_ref[0, blk] = vbT[:, blk * kb_sb:(blk + 1) * kb_sb]
    for blk in range(tm // tq):
        cols = slice(blk * tq, (blk + 1) * tq)
        for hd in range(N_HEADS):
            vTa_ref[0, blk, hd * V_ROWS:hd * V_ROWS + HEAD_DIM, :] = vaT[hd * HEAD_DIM:(hd + 1) * HEAD_DIM, cols]
            vTa_ref[0, blk, hd * V_ROWS + HEAD_DIM:(hd + 1) * V_ROWS, :] = tail


def _proj(x2, g, w_packed, tm, tq, kb_sb, s):
    n, d = x2.shape
    wcols = w_packed.shape[1]
    W = W_ATT
    tiles = s // tm
    nblk = tm // tq
    row = lambda i: (i, 0)
    seq = lambda i: (i // tiles, i % tiles, 0, 0)
    colT = lambda i: (i // tiles, 0, i % tiles)
    return pl.pallas_call(
        functools.partial(_proj_kernel, tq=tq, kb_sb=kb_sb),
        grid=(n // tm,),
        in_specs=[
            pl.BlockSpec((tm, d), row),
            pl.BlockSpec((1, d), lambda i: (0, 0)),
            pl.BlockSpec((d, wcols), lambda i: (0, 0), pipeline_mode=pl.Buffered(1)),
        ],
        out_specs=[
            pl.BlockSpec((tm, W), row),
            pl.BlockSpec((tm, W), row),
            pl.BlockSpec((tm, 2 * d), row),
            pl.BlockSpec((tm, 128), row),
            pl.BlockSpec((1, W, tm), colT),
            pl.BlockSpec((1, W, tm), colT),
            pl.BlockSpec((1, W, tm), colT),
            pl.BlockSpec((1, nblk, N_HEADS * V_ROWS, tq), seq),
            pl.BlockSpec((1, tm // kb_sb, W, kb_sb), seq),
            pl.BlockSpec((1, N_IDX_HEADS, tm), colT),
        ],
        out_shape=[
            jax.ShapeDtypeStruct((n, W), BF16),
            jax.ShapeDtypeStruct((n, W), BF16),
            jax.ShapeDtypeStruct((n, 2 * d), F32),
            jax.ShapeDtypeStruct((n, 128), F32),
            jax.ShapeDtypeStruct((n // s, W, s), BF16),
            jax.ShapeDtypeStruct((n // s, W, s), BF16),
            jax.ShapeDtypeStruct((n // s, W, s), BF16),
            jax.ShapeDtypeStruct((n // s, s // tq, N_HEADS * V_ROWS, tq), BF16),
            jax.ShapeDtypeStruct((n // s, s // kb_sb, W, kb_sb), BF16),
            jax.ShapeDtypeStruct((n // s, N_IDX_HEADS, s), F32),
        ],
        compiler_params=pltpu.CompilerParams(
            dimension_semantics=("arbitrary",), vmem_limit_bytes=VMEM_LIMIT_BYTES),
        name="pre_mix_proj",
    )(x2, g, w_packed)


def _sparse_kernel(qiT_ref, kw_ref, wiT_ref, qT_ref, k_ref, vT_ref, bias_ref, o_ref,
                   key_sc, hi_sc, lo_sc, lo2_sc, msk_sc, lg_sc, mb_sc, m_sc, oT_sc, on_sc, *, tq, topk):
    kb = tq
    i = pl.program_id(1)
    t0 = i * tq
    nkb = i + 1
    row = lax.broadcasted_iota(jnp.int32, (kb, tq), 0)
    col = lax.broadcasted_iota(jnp.int32, (kb, tq), 1)
    w = wiT_ref[0] * (N_IDX_HEADS ** -0.5)

    def blk_slice(j):
        return pl.ds(pl.multiple_of(j * kb, kb), kb)

    def score_block(j):
        ks = blk_slice(j)
        kib = kw_ref[0, ks, 0:IDX_DIM].astype(BF16)
        acc = jnp.zeros((kb, tq), F32)
        for h in range(N_IDX_HEADS):
            d = jnp.dot(kib, qiT_ref[0, h * IDX_DIM:(h + 1) * IDX_DIM, :],
                        preferred_element_type=F32)
            acc = acc + jnp.maximum(d, 0.0) * w[h:h + 1, :]
        adm = ((j * kb + row) // CHUNK) <= ((t0 + col) // CHUNK)
        bits = pltpu.bitcast(acc, jnp.int32)
        key = bits ^ ((bits >> 31) & 0x7FFFFFFF)
        key = jnp.where(adm, key, INT_MIN)
        key_sc[ks, :] = key
        hi_sc[ks, :] = (key >> 16).astype(jnp.int16)
        lo_sc[ks, :] = ((key & 0xFFFF) - 32768).astype(jnp.int16)

    def score_pair(jj, _):
        score_block(2 * jj)
        score_block(jnp.minimum(2 * jj + 1, nkb - 1))
        return 0

    lax.fori_loop(0, (nkb + 1) // 2, score_pair, 0)

    def count16(ref, pred):
        one = jnp.ones((), BF16)
        zero = jnp.zeros((), BF16)

        def body(j, accs):
            hit = jnp.where(pred(ref[blk_slice(j), :]), one, zero).reshape(kb // 16, 16, tq)
            accs = list(accs)
            for r in range(kb // 16):
                accs[r % 4] = accs[r % 4] + hit[r]
            return tuple(accs)

        accs = lax.fori_loop(0, nkb, body, (jnp.zeros((16, tq), BF16),) * 4)
        tot = (accs[0].astype(F32) + accs[1].astype(F32)) + (accs[2].astype(F32) + accs[3].astype(F32))
        return jnp.sum(tot, axis=0, keepdims=True)

    def search16(ref, kth, checkpoints=()):
        def bit_step(b, st):
            thr, taken_at_thr = st
            cand = thr + lax.shift_left(jnp.int32(1), 15 - b)
            cand16 = cand.astype(jnp.int16)
            c = count16(ref, lambda blk: blk >= cand16)
            ok = c >= kth
            return jnp.where(ok, cand, thr), jnp.where(ok, c, taken_at_thr)

        def finish(st, first, rest):
            if not rest:
                return lax.fori_loop(first, 16, bit_step, st)
            st = lax.fori_loop(first, rest[0], bit_step, st)
            unresolved = jnp.max(jnp.where(st[1] != kth, 1.0, 0.0)) > 0.0
            return lax.cond(unresolved, lambda s: finish(s, rest[0], rest[1:]), lambda s: s, st)

        st = (jnp.full((1, tq), -32768, jnp.int32), jnp.full((1, tq), -1.0, F32))
        return finish(st, 0, tuple(checkpoints))[0]

    thr_hi = search16(hi_sc, float(topk))
    thr_hi16 = thr_hi.astype(jnp.int16)

    def low_half_block(j, accs):
        ks = blk_slice(j)
        hi = hi_sc[ks, :]
        lo2_sc[ks, :] = jnp.where(hi == thr_hi16, lo_sc[ks, :], jnp.int16(-32768))
        hit = jnp.where(hi > thr_hi16, jnp.ones((), BF16), jnp.zeros((), BF16)).reshape(kb // 16, 16, tq)
        accs = list(accs)
        for r in range(kb // 16):
            accs[r % 4] = accs[r % 4] + hit[r]
        return tuple(accs)

    accs = lax.fori_loop(0, nkb, low_half_block, (jnp.zeros((16, tq), BF16),) * 4)
    above_hi = jnp.sum((accs[0].astype(F32) + accs[1].astype(F32))
                       + (accs[2].astype(F32) + accs[3].astype(F32)), axis=0, keepdims=True)
    thr_lo = search16(lo2_sc, topk - above_hi, LOW_HALF_CHECKS)
    thr_lo16 = thr_lo.astype(jnp.int16)
    thr = thr_hi * 65536 + (thr_lo + 32768)

    floor = jnp.maximum(thr, INT_MIN + 1)

    def mask_block(j, cnt):
        ks = blk_slice(j)
        sel = key_sc[ks, :] >= floor
        msk_sc[ks, :] = jnp.where(sel, 0.0, MASK_NEG)
        return cnt + jnp.sum(jnp.where(sel, 1.0, 0.0).reshape(kb // 8, 8, tq), axis=0)

    taken = jnp.sum(lax.fori_loop(0, nkb, mask_block, jnp.zeros((8, tq), F32)), axis=0, keepdims=True)
    admissible = ((t0 + col[0:1, :]) // CHUNK + 1) * CHUNK
    spare = jnp.max(jnp.where(taken != jnp.minimum(topk, admissible).astype(F32), 1.0, 0.0)) > 0.0

    @pl.when(spare)
    def _():
        need = topk - (above_hi + count16(lo2_sc, lambda blk: blk > thr_lo16))
        lower = (lax.broadcasted_iota(jnp.int32, (kb, kb), 1)
                 < lax.broadcasted_iota(jnp.int32, (kb, kb), 0)).astype(BF16)

        def ranked_block(j, tie_carry):
            ks = blk_slice(j)
            blk = key_sc[ks, :]
            tie = blk == thr
            tie_f = jnp.where(tie, 1.0, 0.0)
            rank = jnp.dot(lower, tie_f.astype(BF16), preferred_element_type=F32) + tie_carry
            sel = ((blk > thr) | (tie & (rank < need))) & (blk != INT_MIN)
            msk_sc[ks, :] = jnp.where(sel, 0.0, MASK_NEG)
            return tie_carry + jnp.sum(tie_f, axis=0, keepdims=True)

        lax.fori_loop(0, nkb, ranked_block, jnp.zeros((1, tq), F32))

    oT_sc[...] = jnp.zeros_like(oT_sc)
    m_sc[...] = jnp.full((N_HEADS, tq), M_INIT, F32)

    def logits_stage(j, slot):
        ks = blk_slice(j)
        d = jnp.minimum(i - j, 2)
        msk = msk_sc[ks, :]
        for h in range(N_HEADS):
            hs = slice(h * HEAD_DIM, (h + 1) * HEAD_DIM)
            lg = jnp.dot(k_ref[0, ks, hs], qT_ref[0, hs, :],
                         preferred_element_type=F32)
            lg = lg + bias_ref[h, d] + msk
            lg_sc[slot, h] = lg
            mb_sc[slot, h:h + 1, :] = jnp.max(lg, axis=0, keepdims=True)

    def softmax_stage(j, slot):
        m_old = m_sc[...]
        m_new = jnp.maximum(m_old, mb_sc[slot])
        alpha = jnp.exp(m_old - m_new)
        for h in range(N_HEADS):
            vs = slice(h * V_ROWS, (h + 1) * V_ROWS)
            p = jnp.exp(lg_sc[slot, h] - m_new[h:h + 1, :])
            oT_sc[vs, :] = alpha[h:h + 1, :] * oT_sc[vs, :] + jnp.dot(
                vT_ref[0, j, vs, :], p.astype(BF16), preferred_element_type=F32)
        m_sc[...] = m_new

    logits_stage(0, 0)

    def attn_block(j, _):
        slot = j & 1
        softmax_stage(j - 1, 1 - slot)
        logits_stage(j, slot)
        return 0

    lax.fori_loop(1, nkb, attn_block, 0)
    softmax_stage(nkb - 1, (nkb - 1) & 1)
    for h in range(N_HEADS):
        acc = oT_sc[h * V_ROWS:h * V_ROWS + HEAD_DIM, :]
        l = oT_sc[h * V_ROWS + HEAD_DIM:h * V_ROWS + HEAD_DIM + 1, :]
        on_sc[h * HEAD_DIM:(h + 1) * HEAD_DIM, :] = acc / l
    o_ref[0] = on_sc[...].T.astype(BF16)


def _sparse(qiT, kw, wiT, qT, k, vT, bias_tiles, tq, topk):
    b, s, _ = k.shape
    W = W_ATT
    kern = functools.partial(_sparse_kernel, tq=tq, topk=topk)
    return pl.pallas_call(
        kern,
        grid=(b, s // tq),
        in_specs=[
            pl.BlockSpec((1, W, tq), lambda bi, i: (bi, 0, i)),
            pl.BlockSpec((1, s, 128), lambda bi, i: (bi, 0, 0)),
            pl.BlockSpec((1, N_IDX_HEADS, tq), lambda bi, i: (bi, 0, i)),
            pl.BlockSpec((1, W, tq), lambda bi, i: (bi, 0, i)),
            pl.BlockSpec((1, s, W), lambda bi, i: (bi, 0, 0)),
            pl.BlockSpec((1, s // tq, N_HEADS * V_ROWS, tq), lambda bi, i: (bi, 0, 0, 0)),
            pl.BlockSpec((N_HEADS, 3, tq, tq), lambda bi, i: (0, 0, 0, 0),
                         pipeline_mode=pl.Buffered(1)),
        ],
        out_specs=pl.BlockSpec((1, tq, W), lambda bi, i: (bi, i, 0)),
        out_shape=jax.ShapeDtypeStruct((b, s, W), BF16),
        scratch_shapes=[
            pltpu.VMEM((s, tq), jnp.int32),
            pltpu.VMEM((s, tq), jnp.int16),
            pltpu.VMEM((s, tq), jnp.int16),
            pltpu.VMEM((s, tq), jnp.int16),
            pltpu.VMEM((s, tq), F32),
            pltpu.VMEM((2, N_HEADS, tq, tq), F32),
            pltpu.VMEM((2, N_HEADS, tq), F32),
            pltpu.VMEM((N_HEADS, tq), F32),
            pltpu.VMEM((N_HEADS * V_ROWS, tq), F32),
            pltpu.VMEM((W, tq), F32),
        ],
        compiler_params=pltpu.CompilerParams(
            dimension_semantics=("arbitrary", "arbitrary"), vmem_limit_bytes=VMEM_LIMIT_BYTES),
        name="sparse_attention",
    )(qiT, kw, wiT, qT, k, vT, bias_tiles)


def _sb_kernel(qT_ref, k_ref, vT_ref, o_ref, z_sc, hi_sc, lo_sc, a_sc, accT_sc, *, tq, kb):
    i = pl.program_id(1)
    t0 = i * tq
    nb = tq // kb
    row = lax.broadcasted_iota(jnp.int32, (kb, tq), 0)
    col = lax.broadcasted_iota(jnp.int32, (kb, tq), 1)
    upper = (lax.broadcasted_iota(jnp.int32, (kb, kb), 1)
             >= lax.broadcasted_iota(jnp.int32, (kb, kb), 0)).astype(BF16)
    accT_sc[...] = jnp.zeros_like(accT_sc)

    def causal(j):
        return (j * kb + row) < (t0 + col)

    def keep_stage(j, slot, valid, l0=0):
        ks = pl.ds(pl.multiple_of(j * kb, kb), kb)
        sums = []
        for h in range(N_HEADS):
            hs = slice(h * HEAD_DIM, (h + 1) * HEAD_DIM)
            z = jnp.dot(k_ref[0, ks, hs], qT_ref[0, hs, l0:],
                        preferred_element_type=F32)
            lk = -(jnp.maximum(z, 0.0) + jnp.log(1.0 + jnp.exp(-jnp.abs(z))))
            if valid is not None:
                lk = jnp.where(valid[:, l0:], lk, 0.0)
            hi = lk.astype(BF16)
            z_sc[slot, h, :, l0:] = z
            hi_sc[slot, h, :, l0:] = hi
            lo_sc[slot, h, :, l0:] = (lk - hi.astype(F32)).astype(BF16)
            sums.append(jnp.sum(lk, axis=0, keepdims=True))
        sums = jnp.concatenate(sums, axis=0)
        if l0:
            sums = jnp.concatenate([jnp.zeros((N_HEADS, l0), F32), sums], axis=1)
        return sums

    def weight_stage(slot, carry, valid, l0=0):
        for h in range(N_HEADS):
            suf = (jnp.dot(upper, hi_sc[slot, h, :, l0:], preferred_element_type=F32)
                   + jnp.dot(upper, lo_sc[slot, h, :, l0:], preferred_element_type=F32))
            a = jnp.exp(z_sc[slot, h, :, l0:] + suf + carry[h:h + 1, l0:])
            if valid is not None:
                a = jnp.where(valid[:, l0:], a, 0.0)
            a_sc[h, :, l0:] = a.astype(BF16)

    def value_stage(j, when, l0=0):
        @pl.when(when)
        def _():
            for h in range(N_HEADS):
                hs = slice(h * HEAD_DIM, (h + 1) * HEAD_DIM)
                accT_sc[hs, l0:] += jnp.dot(vT_ref[0, j, hs, :], a_sc[h, :, l0:],
                                            preferred_element_type=F32)

    j_top = (i + 1) * nb - 1
    first_lane = [(nb - 1 - r) * kb for r in range(nb)]
    carries = [jnp.zeros((N_HEADS, tq), F32)]
    carries.append(keep_stage(j_top, 0, causal(j_top), first_lane[0]))
    for r in range(1, nb):
        weight_stage((r - 1) % 2, carries[r - 1], causal(j_top - r + 1), first_lane[r - 1])
        carries.append(carries[r] + keep_stage(j_top - r, r % 2, causal(j_top - r), first_lane[r]))
        value_stage(j_top - r + 1, i >= 0, first_lane[r - 1])
    last_slot = (nb - 1) % 2
    j_last = j_top - nb + 1
    c_pend, c_next = carries[nb - 1], carries[nb]

    def only_diagonal():
        weight_stage(last_slot, c_pend, causal(j_last))
        value_stage(j_last, i == 0)

    def more_blocks():
        weight_stage(last_slot, c_pend, causal(j_last))
        c2 = c_next + keep_stage(j_last - 1, 1 - last_slot, None)
        value_stage(j_last, i >= 1)

        def cond(st):
            j, _, _, c_nxt = st
            return jnp.logical_and(j >= 0, jnp.max(c_nxt) > SB_EXIT)

        def body(st):
            j, slot, c_pnd, c_nxt = st
            weight_stage(slot, c_pnd, None)
            sums = keep_stage(j, 1 - slot, None)
            value_stage(j + 1, j >= 0)
            return j - 1, 1 - slot, c_nxt, c_nxt + sums

        j, slot, c_pnd, _ = lax.while_loop(
            cond, body, (j_last - 2, jnp.int32(1 - last_slot), c_next, c2))
        weight_stage(slot, c_pnd, None)
        value_stage(j + 1, i >= 1)

    lax.cond(i >= 1, more_blocks, only_diagonal)
    o_ref[0] = accT_sc[...].T.astype(BF16)


def _stick_breaking(qT, k, vT, tq, kb):
    b, s, _ = k.shape
    W = W_ATT
    kern = functools.partial(_sb_kernel, tq=tq, kb=kb)
    return pl.pallas_call(
        kern,
        grid=(b, s // tq),
        in_specs=[
            pl.BlockSpec((1, W, tq), lambda bi, i: (bi, 0, i)),
            pl.BlockSpec((1, s, W), lambda bi, i: (bi, 0, 0)),
            pl.BlockSpec((1, s // kb, W, kb), lambda bi, i: (bi, 0, 0, 0)),
        ],
        out_specs=pl.BlockSpec((1, tq, W), lambda bi, i: (bi, i, 0)),
        out_shape=jax.ShapeDtypeStruct((b, s, W), BF16),
        scratch_shapes=[
            pltpu.VMEM((2, N_HEADS, kb, tq), F32),
            pltpu.VMEM((2, N_HEADS, kb, tq), BF16),
            pltpu.VMEM((2, N_HEADS, kb, tq), BF16),
            pltpu.VMEM((N_HEADS, kb, tq), BF16),
            pltpu.VMEM((W, tq), F32),
        ],
        compiler_params=pltpu.CompilerParams(
            dimension_semantics=("arbitrary", "arbitrary"), vmem_limit_bytes=VMEM_LIMIT_BYTES),
        name="stick_breaking_attention",
    )(qT, k, vT)


def _merge_ffn_kernel(x_ref, oa_ref, ob_ref, gates_ref, wa_ref, wb_ref, wo_ref, gmix_ref,
                      gpre_ref, wgu_ref, wd_ref, gpost_ref, xo_ref, x1_sc, acc_sc, *, dff, fc):
    d = x_ref.shape[1]
    pa = jnp.dot(oa_ref[...], wa_ref[...], preferred_element_type=F32)
    pb = jnp.dot(ob_ref[...], wb_ref[...], preferred_element_type=F32)
    m = jax.nn.sigmoid(gates_ref[:, 0:d]) * pa + jax.nn.sigmoid(gates_ref[:, d:2 * d]) * pb
    y = jnp.dot(m.astype(BF16), wo_ref[...], preferred_element_type=F32)
    x1_sc[...] = x_ref[...] + _rms(y, gmix_ref[...])
    h = _rms(x1_sc[...], gpre_ref[...]).astype(BF16)
    for c in range(0, dff, fc):
        gate = jnp.dot(h, wgu_ref[:, c:c + fc], preferred_element_type=F32)
        up = jnp.dot(h, wgu_ref[:, dff + c:dff + c + fc], preferred_element_type=F32)
        act = (jax.nn.silu(gate) * up).astype(BF16)
        part = jnp.dot(act, wd_ref[c:c + fc, :], preferred_element_type=F32)
        if c == 0:
            acc_sc[...] = part
        else:
            acc_sc[...] += part
    xo_ref[...] = x1_sc[...] + _rms(acc_sc[...], gpost_ref[...])


def _merge_ffn(x2, oa, ob, gates, wa, wb, wo, gmix, gpre, wgu, wd, gpost, tm, fc):
    n, d = x2.shape
    W = W_ATT
    dff = wd.shape[0]
    row = lambda i: (i, 0)
    const = lambda i: (0, 0)
    resident = functools.partial(pl.BlockSpec, index_map=const, pipeline_mode=pl.Buffered(1))
    kern = functools.partial(_merge_ffn_kernel, dff=dff, fc=fc)
    return pl.pallas_call(
        kern,
        grid=(n // tm,),
        in_specs=[
            pl.BlockSpec((tm, d), row),
            pl.BlockSpec((tm, W), row),
            pl.BlockSpec((tm, W), row),
            pl.BlockSpec((tm, 2 * d), row),
            resident((W, d)),
            resident((W, d)),
            resident((d, d)),
            pl.BlockSpec((1, d), const),
            pl.BlockSpec((1, d), const),
            resident((d, 2 * dff)),
            resident((dff, d)),
            pl.BlockSpec((1, d), const),
        ],
        out_specs=pl.BlockSpec((tm, d), row),
        out_shape=jax.ShapeDtypeStruct((n, d), F32),
        scratch_shapes=[pltpu.VMEM((tm, d), F32), pltpu.VMEM((tm, d), F32)],
        compiler_params=pltpu.CompilerParams(
            dimension_semantics=("arbitrary",), vmem_limit_bytes=VMEM_LIMIT_BYTES),
        name="merge_swiglu",
    )(x2, oa, ob, gates, wa, wb, wo, gmix, gpre, wgu, wd, gpost)


def _t5_bucket(rel):
    nb = N_BUCKETS // 2
    max_exact = nb // 2
    ret = (rel > 0).astype(jnp.int32) * nb
    n = jnp.abs(rel)
    nf = jnp.maximum(n, 1).astype(jnp.float32)
    large = max_exact + (jnp.log(nf / max_exact) / math.log(MAX_DISTANCE / max_exact)
                         * (nb - max_exact)).astype(jnp.int32)
    large = jnp.minimum(large, nb - 1)
    return ret + jnp.where(n < max_exact, n, large)


def _bias_tables(rel_bias, tq):
    s_l = jnp.arange(tq, dtype=jnp.int32)[:, None]
    t_l = jnp.arange(tq, dtype=jnp.int32)[None, :]
    rel = jnp.stack([s_l - t_l - d * tq for d in range(3)])
    onehot = (_t5_bucket(rel)[..., None] == jnp.arange(N_BUCKETS)).astype(F32)
    return jnp.einsum("dstb,bh->hdst", onehot, rel_bias.astype(F32), precision=lax.Precision.HIGHEST)


def _pack_w_in(w):
    d = w.shape[0]
    head = 3 * W_ATT + N_IDX_HEADS * IDX_DIM
    small = IDX_DIM + N_IDX_HEADS
    pad = jnp.zeros((d, 128 - small), w.dtype)
    return jnp.concatenate([w[:, :head], w[:, head + small:], w[:, head:head + small], pad],
                           axis=1).astype(BF16)


def kernel(x, w_in, w_branch_sparse, w_branch_sb, w_out, w_gate_up, w_down,
           g_pre_mix, g_post_mix, g_pre_ffn, g_post_ffn, rel_bias):
    b, s, d = x.shape
    depth = w_in.shape[0]
    n = b * s
    W = W_ATT
    tm = min(512, s)
    tq = min(256, s)
    kb_sb = min(128, tq)
    topk = min(INDEX_TOPK, s // 4)
    dff = w_down.shape[1]
    fc = 256
    assert s % tm == 0 and tm % tq == 0 and dff % fc == 0 and tq % CHUNK == 0

    bias_tiles = _bias_tables(rel_bias, tq)
    x2 = x.reshape(n, d)
    for l in range(depth):
        ka, kb, gates, kw, qaT, qiT, qbT, vTa, vTb, wiT = _proj(
            x2, g_pre_mix[l][None], _pack_w_in(w_in[l]), tm, tq, kb_sb, s)
        o_a = _sparse(qiT, kw.reshape(b, s, 128), wiT, qaT, ka.reshape(b, s, W), vTa,
                      bias_tiles, tq, topk)
        o_b = _stick_breaking(qbT, kb.reshape(b, s, W), vTb, tq, kb_sb)
        x2 = _merge_ffn(x2, o_a.reshape(n, W), o_b.reshape(n, W), gates,
                        w_branch_sparse[l].astype(BF16), w_branch_sb[l].astype(BF16),
                        w_out[l].astype(BF16), g_post_mix[l][None], g_pre_ffn[l][None],
                        w_gate_up[l].astype(BF16), w_down[l].astype(BF16), g_post_ffn[l][None], tm, fc)
    return x2.reshape(b, s, d)
```

```python
import functools
import math

import jax
import jax.numpy as jnp
from jax import lax
from jax.experimental import pallas as pl
from jax.experimental.pallas import tpu as pltpu

F32 = jnp.float32
BF16 = jnp.bfloat16

CHUNK = 64
HEAD_DIM = 64
N_HEADS = 8
W_ATT = N_HEADS * HEAD_DIM
N_IDX_HEADS = 8
IDX_DIM = 64
INDEX_TOPK = 256
N_BUCKETS = 32
MAX_DISTANCE = 128
RMS_EPS = 1e-6

INT_MIN = -(2 ** 31)
MASK_NEG = -1e30
M_INIT = -1e29
V_ROWS = HEAD_DIM + 16
LOW_HALF_CHECKS = (10, 13)
SB_EXIT = -88.0

VMEM_LIMIT_BYTES = 58 * 1024 * 1024


def _rms(x, g):
    ms = jnp.mean(x * x, axis=-1, keepdims=True)
    return x * lax.rsqrt(ms + RMS_EPS) * g


def _proj_kernel(x_ref, g_ref, w_ref, ka_ref, kb_ref, gates_ref, kw_ref,
                 qaT_ref, qiT_ref, qbT_ref, vTa_ref, vTb_ref, wiT_ref, *, tq, kb_sb):
    h = _rms(x_ref[...], g_ref[...]).astype(BF16)
    tm, d = x_ref.shape
    W = W_ATT

    def mm(c0, c1):
        return jnp.dot(h, w_ref[:, c0:c1], preferred_element_type=F32)

    qaT_ref[0] = (mm(0, W) * 0.125).T.astype(BF16)
    ka_ref[...] = mm(W, 2 * W).astype(BF16)
    qiT_ref[0] = (mm(3 * W, 4 * W) * 0.125).T.astype(BF16)
    qbT_ref[0] = (mm(4 * W, 5 * W) * 0.125).T.astype(BF16)
    kb_ref[...] = mm(5 * W, 6 * W).astype(BF16)
    for c in range(0, 2 * d, W):
        gates_ref[:, c:c + W] = mm(7 * W + c, 7 * W + c + W)
    kw = mm(7 * W + 2 * d, 7 * W + 2 * d + 128)
    kw_ref[...] = kw
    wiT_ref[0] = kw.T[IDX_DIM:IDX_DIM + N_IDX_HEADS, :]

    vaT = mm(2 * W, 3 * W).T.astype(BF16)
    vbT = mm(6 * W, 7 * W).T.astype(BF16)
    tail = (lax.broadcasted_iota(jnp.int32, (V_ROWS - HEAD_DIM, tq), 0) == 0).astype(BF16)
    for blk in range(tm // kb_sb):
        vTb_ref[0, blk] = vbT[:, blk * kb_sb:(blk + 1) * kb_sb]
    for blk in range(tm // tq):
        cols = slice(blk * tq, (blk + 1) * tq)
        for hd in range(N_HEADS):
            vTa_ref[0, blk, hd * V_ROWS:hd * V_ROWS + HEAD_DIM, :] = vaT[hd * HEAD_DIM:(hd + 1) * HEAD_DIM, cols]
            vTa_ref[0, blk, hd * V_ROWS + HEAD_DIM:(hd + 1) * V_ROWS, :] = tail


def _proj(x2, g, w_packed, tm, tq, kb_sb, s):
    n, d = x2.shape
    wcols = w_packed.shape[1]
    W = W_ATT
    tiles = s // tm
    nblk = tm // tq
    row = lambda i: (i, 0)
    seq = lambda i: (i // tiles, i % tiles, 0, 0)
    colT = lambda i: (i // tiles, 0, i % tiles)
    return pl.pallas_call(
        functools.partial(_proj_kernel, tq=tq, kb_sb=kb_sb),
        grid=(n // tm,),
        in_specs=[
            pl.BlockSpec((tm, d), row),
            pl.BlockSpec((1, d), lambda i: (0, 0)),
            pl.BlockSpec((d, wcols), lambda i: (0, 0), pipeline_mode=pl.Buffered(1)),
        ],
        out_specs=[
            pl.BlockSpec((tm, W), row),
            pl.BlockSpec((tm, W), row),
            pl.BlockSpec((tm, 2 * d), row),
            pl.BlockSpec((tm, 128), row),
            pl.BlockSpec((1, W, tm), colT),
            pl.BlockSpec((1, W, tm), colT),
            pl.BlockSpec((1, W, tm), colT),
            pl.BlockSpec((1, nblk, N_HEADS * V_ROWS, tq), seq),
            pl.BlockSpec((1, tm // kb_sb, W, kb_sb), seq),
            pl.BlockSpec((1, N_IDX_HEADS, tm), colT),
        ],
        out_shape=[
            jax.ShapeDtypeStruct((n, W), BF16),
            jax.ShapeDtypeStruct((n, W), BF16),
            jax.ShapeDtypeStruct((n, 2 * d), F32),
            jax.ShapeDtypeStruct((n, 128), F32),
            jax.ShapeDtypeStruct((n // s, W, s), BF16),
            jax.ShapeDtypeStruct((n // s, W, s), BF16),
            jax.ShapeDtypeStruct((n // s, W, s), BF16),
            jax.ShapeDtypeStruct((n // s, s // tq, N_HEADS * V_ROWS, tq), BF16),
            jax.ShapeDtypeStruct((n // s, s // kb_sb, W, kb_sb), BF16),
            jax.ShapeDtypeStruct((n // s, N_IDX_HEADS, s), F32),
        ],
        compiler_params=pltpu.CompilerParams(
            dimension_semantics=("arbitrary",), vmem_limit_bytes=VMEM_LIMIT_BYTES),
        name="pre_mix_proj",
    )(x2, g, w_packed)


def _sparse_kernel(qiT_ref, kw_ref, wiT_ref, qT_ref, k_ref, vT_ref, bias_ref, o_ref,
                   key_sc, hi_sc, lo_sc, lo2_sc, msk_sc, lg_sc, mb_sc, m_sc, oT_sc, on_sc, *, tq, topk):
    kb = tq
    i = pl.program_id(1)
    t0 = i * tq
    nkb = i + 1
    row = lax.broadcasted_iota(jnp.int32, (kb, tq), 0)
    col = lax.broadcasted_iota(jnp.int32, (kb, tq), 1)
    w = wiT_ref[0] * (N_IDX_HEADS ** -0.5)

    def blk_slice(j):
        return pl.ds(pl.multiple_of(j * kb, kb), kb)

    def score_block(j):
        ks = blk_slice(j)
        kib = kw_ref[0, ks, 0:IDX_DIM].astype(BF16)
        acc = jnp.zeros((kb, tq), F32)
        for h in range(N_IDX_HEADS):
            d = jnp.dot(kib, qiT_ref[0, h * IDX_DIM:(h + 1) * IDX_DIM, :],
                        preferred_element_type=F32)
            acc = acc + jnp.maximum(d, 0.0) * w[h:h + 1, :]
        adm = ((j * kb + row) // CHUNK) <= ((t0 + col) // CHUNK)
        bits = pltpu.bitcast(acc, jnp.int32)
        key = bits ^ ((bits >> 31) & 0x7FFFFFFF)
        key = jnp.where(adm, key, INT_MIN)
        key_sc[ks, :] = key
        hi_sc[ks, :] = (key >> 16).astype(jnp.int16)
        lo_sc[ks, :] = ((key & 0xFFFF) - 32768).astype(jnp.int16)

    def score_pair(jj, _):
        score_block(2 * jj)
        score_block(jnp.minimum(2 * jj + 1, nkb - 1))
        return 0

    lax.fori_loop(0, (nkb + 1) // 2, score_pair, 0)

    def count16(ref, pred):
        one = jnp.ones((), BF16)
        zero = jnp.zeros((), BF16)

        def body(j, accs):
            hit = jnp.where(pred(ref[blk_slice(j), :]), one, zero).reshape(kb // 16, 16, tq)
            accs = list(accs)
            for r in range(kb // 16):
                accs[r % 4] = accs[r % 4] + hit[r]
            return tuple(accs)

        accs = lax.fori_loop(0, nkb, body, (jnp.zeros((16, tq), BF16),) * 4)
        tot = (accs[0].astype(F32) + accs[1].astype(F32)) + (accs[2].astype(F32) + accs[3].astype(F32))
        return jnp.sum(tot, axis=0, keepdims=True)

    def search16(ref, kth, checkpoints=()):
        def bit_step(b, st):
            thr, taken_at_thr = st
            cand = thr + lax.shift_left(jnp.int32(1), 15 - b)
            cand16 = cand.astype(jnp.int16)
            c = count16(ref, lambda blk: blk >= cand16)
            ok = c >= kth
            return jnp.where(ok, cand, thr), jnp.where(ok, c, taken_at_thr)

        def finish(st, first, rest):
            if not rest:
                return lax.fori_loop(first, 16, bit_step, st)
            st = lax.fori_loop(first, rest[0], bit_step, st)
            unresolved = jnp.max(jnp.where(st[1] != kth, 1.0, 0.0)) > 0.0
            return lax.cond(unresolved, lambda s: finish(s, rest[0], rest[1:]), lambda s: s, st)

        st = (jnp.full((1, tq), -32768, jnp.int32), jnp.full((1, tq), -1.0, F32))
        return finish(st, 0, tuple(checkpoints))

    thr_hi, _ = search16(hi_sc, float(topk))
    thr_hi16 = thr_hi.astype(jnp.int16)

    def low_half_block(j, accs):
        ks = blk_slice(j)
        hi = hi_sc[ks, :]
        lo2_sc[ks, :] = jnp.where(hi == thr_hi16, lo_sc[ks, :], jnp.int16(-32768))
        hit = jnp.where(hi > thr_hi16, jnp.ones((), BF16), jnp.zeros((), BF16)).reshape(kb // 16, 16, tq)
        accs = list(accs)
        for r in range(kb // 16):
            accs[r % 4] = accs[r % 4] + hit[r]
        return tuple(accs)

    accs = lax.fori_loop(0, nkb, low_half_block, (jnp.zeros((16, tq), BF16),) * 4)
    above_hi = jnp.sum((accs[0].astype(F32) + accs[1].astype(F32))
                       + (accs[2].astype(F32) + accs[3].astype(F32)), axis=0, keepdims=True)
    thr_lo, taken_low = search16(lo2_sc, topk - above_hi, LOW_HALF_CHECKS)
    thr_lo16 = thr_lo.astype(jnp.int16)
    thr = thr_hi * 65536 + (thr_lo + 32768)

    floor = jnp.maximum(thr, INT_MIN + 1)

    def mask_block(j, _):
        ks = blk_slice(j)
        msk_sc[ks, :] = jnp.where(key_sc[ks, :] >= floor, 0.0, MASK_NEG)
        return 0

    lax.fori_loop(0, nkb, mask_block, 0)
    spare = jnp.max(jnp.where(taken_low != topk - above_hi, 1.0, 0.0)) > 0.0

    @pl.when(spare)
    def _():
        need = topk - (above_hi + count16(lo2_sc, lambda blk: blk > thr_lo16))
        lower = (lax.broadcasted_iota(jnp.int32, (kb, kb), 1)
                 < lax.broadcasted_iota(jnp.int32, (kb, kb), 0)).astype(BF16)

        def ranked_block(j, tie_carry):
            ks = blk_slice(j)
            blk = key_sc[ks, :]
            tie = blk == thr
            tie_f = jnp.where(tie, 1.0, 0.0)
            rank = jnp.dot(lower, tie_f.astype(BF16), preferred_element_type=F32) + tie_carry
            sel = ((blk > thr) | (tie & (rank < need))) & (blk != INT_MIN)
            msk_sc[ks, :] = jnp.where(sel, 0.0, MASK_NEG)
            return tie_carry + jnp.sum(tie_f, axis=0, keepdims=True)

        lax.fori_loop(0, nkb, ranked_block, jnp.zeros((1, tq), F32))

    oT_sc[...] = jnp.zeros_like(oT_sc)
    m_sc[...] = jnp.full((N_HEADS, tq), M_INIT, F32)

    def logits_stage(j, slot):
        ks = blk_slice(j)
        d = jnp.minimum(i - j, 2)
        msk = msk_sc[ks, :]
        for h in range(N_HEADS):
            hs = slice(h * HEAD_DIM, (h + 1) * HEAD_DIM)
            lg = jnp.dot(k_ref[0, ks, hs], qT_ref[0, hs, :],
                         preferred_element_type=F32)
            lg = lg + bias_ref[h, d] + msk
            lg_sc[slot, h] = lg
            mb_sc[slot, h:h + 1, :] = jnp.max(lg, axis=0, keepdims=True)

    def softmax_stage(j, slot):
        m_old = m_sc[...]
        m_new = jnp.maximum(m_old, mb_sc[slot])
        alpha = jnp.exp(m_old - m_new)
        for h in range(N_HEADS):
            vs = slice(h * V_ROWS, (h + 1) * V_ROWS)
            p = jnp.exp(lg_sc[slot, h] - m_new[h:h + 1, :])
            oT_sc[vs, :] = alpha[h:h + 1, :] * oT_sc[vs, :] + jnp.dot(
                vT_ref[0, j, vs, :], p.astype(BF16), preferred_element_type=F32)
        m_sc[...] = m_new

    logits_stage(0, 0)

    def attn_block(j, _):
        slot = j & 1
        softmax_stage(j - 1, 1 - slot)
        logits_stage(j, slot)
        return 0

    lax.fori_loop(1, nkb, attn_block, 0)
    softmax_stage(nkb - 1, (nkb - 1) & 1)
    for h in range(N_HEADS):
        acc = oT_sc[h * V_ROWS:h * V_ROWS + HEAD_DIM, :]
        l = oT_sc[h * V_ROWS + HEAD_DIM:h * V_ROWS + HEAD_DIM + 1, :]
        on_sc[h * HEAD_DIM:(h + 1) * HEAD_DIM, :] = acc / l
    o_ref[0] = on_sc[...].T.astype(BF16)


def _sparse(qiT, kw, wiT, qT, k, vT, bias_tiles, tq, topk):
    b, s, _ = k.shape
    W = W_ATT
    kern = functools.partial(_sparse_kernel, tq=tq, topk=topk)
    return pl.pallas_call(
        kern,
        grid=(b, s // tq),
        in_specs=[
            pl.BlockSpec((1, W, tq), lambda bi, i: (bi, 0, i)),
            pl.BlockSpec((1, s, 128), lambda bi, i: (bi, 0, 0)),
            pl.BlockSpec((1, N_IDX_HEADS, tq), lambda bi, i: (bi, 0, i)),
            pl.BlockSpec((1, W, tq), lambda bi, i: (bi, 0, i)),
            pl.BlockSpec((1, s, W), lambda bi, i: (bi, 0, 0)),
            pl.BlockSpec((1, s // tq, N_HEADS * V_ROWS, tq), lambda bi, i: (bi, 0, 0, 0)),
            pl.BlockSpec((N_HEADS, 3, tq, tq), lambda bi, i: (0, 0, 0, 0),
                         pipeline_mode=pl.Buffered(1)),
        ],
        out_specs=pl.BlockSpec((1, tq, W), lambda bi, i: (bi, i, 0)),
        out_shape=jax.ShapeDtypeStruct((b, s, W), BF16),
        scratch_shapes=[
            pltpu.VMEM((s, tq), jnp.int32),
            pltpu.VMEM((s, tq), jnp.int16),
            pltpu.VMEM((s, tq), jnp.int16),
            pltpu.VMEM((s, tq), jnp.int16),
            pltpu.VMEM((s, tq), F32),
            pltpu.VMEM((2, N_HEADS, tq, tq), F32),
            pltpu.VMEM((2, N_HEADS, tq), F32),
            pltpu.VMEM((N_HEADS, tq), F32),
            pltpu.VMEM((N_HEADS * V_ROWS, tq), F32),
            pltpu.VMEM((W, tq), F32),
        ],
        compiler_params=pltpu.CompilerParams(
            dimension_semantics=("arbitrary", "arbitrary"), vmem_limit_bytes=VMEM_LIMIT_BYTES),
        name="sparse_attention",
    )(qiT, kw, wiT, qT, k, vT, bias_tiles)


def _sb_kernel(qT_ref, k_ref, vT_ref, o_ref, z_sc, hi_sc, lo_sc, a_sc, accT_sc, *, tq, kb):
    i = pl.program_id(1)
    t0 = i * tq
    nb = tq // kb
    row = lax.broadcasted_iota(jnp.int32, (kb, tq), 0)
    col = lax.broadcasted_iota(jnp.int32, (kb, tq), 1)
    upper = (lax.broadcasted_iota(jnp.int32, (kb, kb), 1)
             >= lax.broadcasted_iota(jnp.int32, (kb, kb), 0)).astype(BF16)
    accT_sc[...] = jnp.zeros_like(accT_sc)

    def causal(j):
        return (j * kb + row) < (t0 + col)

    def keep_stage(j, slot, valid, l0=0):
        ks = pl.ds(pl.multiple_of(j * kb, kb), kb)
        sums = []
        for h in range(N_HEADS):
            hs = slice(h * HEAD_DIM, (h + 1) * HEAD_DIM)
            z = jnp.dot(k_ref[0, ks, hs], qT_ref[0, hs, l0:],
                        preferred_element_type=F32)
            lk = -(jnp.maximum(z, 0.0) + jnp.log(1.0 + jnp.exp(-jnp.abs(z))))
            if valid is not None:
                lk = jnp.where(valid[:, l0:], lk, 0.0)
            hi = lk.astype(BF16)
            z_sc[slot, h, :, l0:] = z
            hi_sc[slot, h, :, l0:] = hi
            lo_sc[slot, h, :, l0:] = (lk - hi.astype(F32)).astype(BF16)
            sums.append(jnp.sum(lk, axis=0, keepdims=True))
        sums = jnp.concatenate(sums, axis=0)
        if l0:
            sums = jnp.concatenate([jnp.zeros((N_HEADS, l0), F32), sums], axis=1)
        return sums

    def weight_stage(slot, carry, valid, l0=0):
        for h in range(N_HEADS):
            suf = (jnp.dot(upper, hi_sc[slot, h, :, l0:], preferred_element_type=F32)
                   + jnp.dot(upper, lo_sc[slot, h, :, l0:], preferred_element_type=F32))
            a = jnp.exp(z_sc[slot, h, :, l0:] + suf + carry[h:h + 1, l0:])
            if valid is not None:
                a = jnp.where(valid[:, l0:], a, 0.0)
            a_sc[h, :, l0:] = a.astype(BF16)

    def value_stage(j, when, l0=0):
        @pl.when(when)
        def _():
            for h in range(N_HEADS):
                hs = slice(h * HEAD_DIM, (h + 1) * HEAD_DIM)
                accT_sc[hs, l0:] += jnp.dot(vT_ref[0, j, hs, :], a_sc[h, :, l0:],
                                            preferred_element_type=F32)

    j_top = (i + 1) * nb - 1
    first_lane = [(nb - 1 - r) * kb for r in range(nb)]
    carries = [jnp.zeros((N_HEADS, tq), F32)]
    carries.append(keep_stage(j_top, 0, causal(j_top), first_lane[0]))
    for r in range(1, nb):
        weight_stage((r - 1) % 2, carries[r - 1], causal(j_top - r + 1), first_lane[r - 1])
        carries.append(carries[r] + keep_stage(j_top - r, r % 2, causal(j_top - r), first_lane[r]))
        value_stage(j_top - r + 1, i >= 0, first_lane[r - 1])
    last_slot = (nb - 1) % 2
    j_last = j_top - nb + 1
    c_pend, c_next = carries[nb - 1], carries[nb]

    def only_diagonal():
        weight_stage(last_slot, c_pend, causal(j_last))
        value_stage(j_last, i == 0)

    def more_blocks():
        weight_stage(last_slot, c_pend, causal(j_last))
        c2 = c_next + keep_stage(j_last - 1, 1 - last_slot, None)
        value_stage(j_last, i >= 1)

        def cond(st):
            j, _, _, c_nxt = st
            return jnp.logical_and(j >= 0, jnp.max(c_nxt) > SB_EXIT)

        def body(st):
            j, slot, c_pnd, c_nxt = st
            weight_stage(slot, c_pnd, None)
            sums = keep_stage(j, 1 - slot, None)
            value_stage(j + 1, j >= 0)
            return j - 1, 1 - slot, c_nxt, c_nxt + sums

        j, slot, c_pnd, _ = lax.while_loop(
            cond, body, (j_last - 2, jnp.int32(1 - last_slot), c_next, c2))
        weight_stage(slot, c_pnd, None)
        value_stage(j + 1, i >= 1)

    lax.cond(i >= 1, more_blocks, only_diagonal)
    o_ref[0] = accT_sc[...].T.astype(BF16)


def _stick_breaking(qT, k, vT, tq, kb):
    b, s, _ = k.shape
    W = W_ATT
    kern = functools.partial(_sb_kernel, tq=tq, kb=kb)
    return pl.pallas_call(
        kern,
        grid=(b, s // tq),
        in_specs=[
            pl.BlockSpec((1, W, tq), lambda bi, i: (bi, 0, i)),
            pl.BlockSpec((1, s, W), lambda bi, i: (bi, 0, 0)),
            pl.BlockSpec((1, s // kb, W, kb), lambda bi, i: (bi, 0, 0, 0)),
        ],
        out_specs=pl.BlockSpec((1, tq, W), lambda bi, i: (bi, i, 0)),
        out_shape=jax.ShapeDtypeStruct((b, s, W), BF16),
        scratch_shapes=[
            pltpu.VMEM((2, N_HEADS, kb, tq), F32),
            pltpu.VMEM((2, N_HEADS, kb, tq), BF16),
            pltpu.VMEM((2, N_HEADS, kb, tq), BF16),
            pltpu.VMEM((N_HEADS, kb, tq), BF16),
            pltpu.VMEM((W, tq), F32),
        ],
        compiler_params=pltpu.CompilerParams(
            dimension_semantics=("arbitrary", "arbitrary"), vmem_limit_bytes=VMEM_LIMIT_BYTES),
        name="stick_breaking_attention",
    )(qT, k, vT)


def _merge_ffn_kernel(x_ref, oa_ref, ob_ref, gates_ref, wa_ref, wb_ref, wo_ref, gmix_ref,
                      gpre_ref, wgu_ref, wd_ref, gpost_ref, xo_ref, x1_sc, acc_sc, *, dff, fc):
    d = x_ref.shape[1]
    pa = jnp.dot(oa_ref[...], wa_ref[...], preferred_element_type=F32)
    pb = jnp.dot(ob_ref[...], wb_ref[...], preferred_element_type=F32)
    m = jax.nn.sigmoid(gates_ref[:, 0:d]) * pa + jax.nn.sigmoid(gates_ref[:, d:2 * d]) * pb
    y = jnp.dot(m.astype(BF16), wo_ref[...], preferred_element_type=F32)
    x1_sc[...] = x_ref[...] + _rms(y, gmix_ref[...])
    h = _rms(x1_sc[...], gpre_ref[...]).astype(BF16)
    for c in range(0, dff, fc):
        gate = jnp.dot(h, wgu_ref[:, c:c + fc], preferred_element_type=F32)
        up = jnp.dot(h, wgu_ref[:, dff + c:dff + c + fc], preferred_element_type=F32)
        act = (jax.nn.silu(gate) * up).astype(BF16)
        part = jnp.dot(act, wd_ref[c:c + fc, :], preferred_element_type=F32)
        if c == 0:
            acc_sc[...] = part
        else:
            acc_sc[...] += part
    xo_ref[...] = x1_sc[...] + _rms(acc_sc[...], gpost_ref[...])


def _merge_ffn(x2, oa, ob, gates, wa, wb, wo, gmix, gpre, wgu, wd, gpost, tm, fc):
    n, d = x2.shape
    W = W_ATT
    dff = wd.shape[0]
    row = lambda i: (i, 0)
    const = lambda i: (0, 0)
    resident = functools.partial(pl.BlockSpec, index_map=const, pipeline_mode=pl.Buffered(1))
    kern = functools.partial(_merge_ffn_kernel, dff=dff, fc=fc)
    return pl.pallas_call(
        kern,
        grid=(n // tm,),
        in_specs=[
            pl.BlockSpec((tm, d), row),
            pl.BlockSpec((tm, W), row),
            pl.BlockSpec((tm, W), row),
            pl.BlockSpec((tm, 2 * d), row),
            resident((W, d)),
            resident((W, d)),
            resident((d, d)),
            pl.BlockSpec((1, d), const),
            pl.BlockSpec((1, d), const),
            resident((d, 2 * dff)),
            resident((dff, d)),
            pl.BlockSpec((1, d), const),
        ],
        out_specs=pl.BlockSpec((tm, d), row),
        out_shape=jax.ShapeDtypeStruct((n, d), F32),
        scratch_shapes=[pltpu.VMEM((tm, d), F32), pltpu.VMEM((tm, d), F32)],
        compiler_params=pltpu.CompilerParams(
            dimension_semantics=("arbitrary",), vmem_limit_bytes=VMEM_LIMIT_BYTES),
        name="merge_swiglu",
    )(x2, oa, ob, gates, wa, wb, wo, gmix, gpre, wgu, wd, gpost)


def _t5_bucket(rel):
    nb = N_BUCKETS // 2
    max_exact = nb // 2
    ret = (rel > 0).astype(jnp.int32) * nb
    n = jnp.abs(rel)
    nf = jnp.maximum(n, 1).astype(jnp.float32)
    large = max_exact + (jnp.log(nf / max_exact) / math.log(MAX_DISTANCE / max_exact)
                         * (nb - max_exact)).astype(jnp.int32)
    large = jnp.minimum(large, nb - 1)
    return ret + jnp.where(n < max_exact, n, large)


def _bias_tables(rel_bias, tq):
    s_l = jnp.arange(tq, dtype=jnp.int32)[:, None]
    t_l = jnp.arange(tq, dtype=jnp.int32)[None, :]
    rel = jnp.stack([s_l - t_l - d * tq for d in range(3)])
    onehot = (_t5_bucket(rel)[..., None] == jnp.arange(N_BUCKETS)).astype(F32)
    return jnp.einsum("dstb,bh->hdst", onehot, rel_bias.astype(F32), precision=lax.Precision.HIGHEST)


def _pack_w_in(w):
    d = w.shape[0]
    head = 3 * W_ATT + N_IDX_HEADS * IDX_DIM
    small = IDX_DIM + N_IDX_HEADS
    pad = jnp.zeros((d, 128 - small), w.dtype)
    return jnp.concatenate([w[:, :head], w[:, head + small:], w[:, head:head + small], pad],
                           axis=1).astype(BF16)


def kernel(x, w_in, w_branch_sparse, w_branch_sb, w_out, w_gate_up, w_down,
           g_pre_mix, g_post_mix, g_pre_ffn, g_post_ffn, rel_bias):
    b, s, d = x.shape
    depth = w_in.shape[0]
    n = b * s
    W = W_ATT
    tm = min(512, s)
    tq = min(256, s)
    kb_sb = min(128, tq)
    topk = min(INDEX_TOPK, s // 4)
    dff = w_down.shape[1]
    fc = 256
    assert s % tm == 0 and tm % tq == 0 and dff % fc == 0 and tq % CHUNK == 0

    bias_tiles = _bias_tables(rel_bias, tq)
    x2 = x.reshape(n, d)
    for l in range(depth):
        ka, kb, gates, kw, qaT, qiT, qbT, vTa, vTb, wiT = _proj(
            x2, g_pre_mix[l][None], _pack_w_in(w_in[l]), tm, tq, kb_sb, s)
        o_a = _sparse(qiT, kw.reshape(b, s, 128), wiT, qaT, ka.reshape(b, s, W), vTa,
                      bias_tiles, tq, topk)
        o_b = _stick_breaking(qbT, kb.reshape(b, s, W), vTb, tq, kb_sb)
        x2 = _merge_ffn(x2, o_a.reshape(n, W), o_b.reshape(n, W), gates,
                        w_branch_sparse[l].astype(BF16), w_branch_sb[l].astype(BF16),
                        w_out[l].astype(BF16), g_post_mix[l][None], g_pre_ffn[l][None],
                        w_gate_up[l].astype(BF16), w_down[l].astype(BF16), g_post_ffn[l][None], tm, fc)
    return x2.reshape(b, s, d)
```

```python
import functools
import math

import jax
import jax.numpy as jnp
from jax import lax
from jax.experimental import pallas as pl
from jax.experimental.pallas import tpu as pltpu

F32 = jnp.float32
BF16 = jnp.bfloat16

CHUNK = 64
HEAD_DIM = 64
N_HEADS = 8
W_ATT = N_HEADS * HEAD_DIM
N_IDX_HEADS = 8
IDX_DIM = 64
INDEX_TOPK = 256
N_BUCKETS = 32
MAX_DISTANCE = 128
RMS_EPS = 1e-6

INT_MIN = -(2 ** 31)
MASK_NEG = -1e30
M_INIT = -1e29
V_ROWS = HEAD_DIM + 16
LOW_HALF_CHECKS = (10, 13)
SB_EXIT = -88.0

VMEM_LIMIT_BYTES = 58 * 1024 * 1024


def _rms(x, g):
    ms = jnp.mean(x * x, axis=-1, keepdims=True)
    return x * lax.rsqrt(ms + RMS_EPS) * g


def _proj_kernel(x_ref, g_ref, w_ref, ka_ref, kb_ref, gates_ref, kw_ref,
                 qaT_ref, qiT_ref, qbT_ref, vTa_ref, vTb_ref, wiT_ref, *, tq, kb_sb):
    h = _rms(x_ref[...], g_ref[...]).astype(BF16)
    tm, d = x_ref.shape
    W = W_ATT

    def mm(c0, c1):
        return jnp.dot(h, w_ref[:, c0:c1], preferred_element_type=F32)

    qaT_ref[0] = (mm(0, W) * 0.125).T.astype(BF16)
    ka_ref[...] = mm(W, 2 * W).astype(BF16)
    qiT_ref[0] = (mm(3 * W, 4 * W) * 0.125).T.astype(BF16)
    qbT_ref[0] = (mm(4 * W, 5 * W) * 0.125).T.astype(BF16)
    kb_ref[...] = mm(5 * W, 6 * W).astype(BF16)
    for c in range(0, 2 * d, W):
        gates_ref[:, c:c + W] = mm(7 * W + c, 7 * W + c + W)
    kw = mm(7 * W + 2 * d, 7 * W + 2 * d + 128)
    kw_ref[...] = kw
    wiT_ref[0] = kw.T[IDX_DIM:IDX_DIM + N_IDX_HEADS, :]

    vaT = mm(2 * W, 3 * W).T.astype(BF16)
    vbT = mm(6 * W, 7 * W).T.astype(BF16)
    tail = (lax.broadcasted_iota(jnp.int32, (V_ROWS - HEAD_DIM, tq), 0) == 0).astype(BF16)
    for blk in range(tm // kb_sb):
        vTb_ref[0, blk] = vbT[:, blk * kb_sb:(blk + 1) * kb_sb]
    for blk in range(tm // tq):
        cols = slice(blk * tq, (blk + 1) * tq)
        for hd in range(N_HEADS):
            vTa_ref[0, blk, hd * V_ROWS:hd * V_ROWS + HEAD_DIM, :] = vaT[hd * HEAD_DIM:(hd + 1) * HEAD_DIM, cols]
            vTa_ref[0, blk, hd * V_ROWS + HEAD_DIM:(hd + 1) * V_ROWS, :] = tail


def _proj(x2, g, w_packed, tm, tq, kb_sb, s):
    n, d = x2.shape
    wcols = w_packed.shape[1]
    W = W_ATT
    tiles = s // tm
    nblk = tm // tq
    row = lambda i: (i, 0)
    seq = lambda i: (i // tiles, i % tiles, 0, 0)
    colT = lambda i: (i // tiles, 0, i % tiles)
    return pl.pallas_call(
        functools.partial(_proj_kernel, tq=tq, kb_sb=kb_sb),
        grid=(n // tm,),
        in_specs=[
            pl.BlockSpec((tm, d), row),
            pl.BlockSpec((1, d), lambda i: (0, 0)),
            pl.BlockSpec((d, wcols), lambda i: (0, 0), pipeline_mode=pl.Buffered(1)),
        ],
        out_specs=[
            pl.BlockSpec((tm, W), row),
            pl.BlockSpec((tm, W), row),
            pl.BlockSpec((tm, 2 * d), row),
            pl.BlockSpec((tm, 128), row),
            pl.BlockSpec((1, W, tm), colT),
            pl.BlockSpec((1, W, tm), colT),
            pl.BlockSpec((1, W, tm), colT),
            pl.BlockSpec((1, nblk, N_HEADS * V_ROWS, tq), seq),
            pl.BlockSpec((1, tm // kb_sb, W, kb_sb), seq),
            pl.BlockSpec((1, N_IDX_HEADS, tm), colT),
        ],
        out_shape=[
            jax.ShapeDtypeStruct((n, W), BF16),
            jax.ShapeDtypeStruct((n, W), BF16),
            jax.ShapeDtypeStruct((n, 2 * d), F32),
            jax.ShapeDtypeStruct((n, 128), F32),
            jax.ShapeDtypeStruct((n // s, W, s), BF16),
            jax.ShapeDtypeStruct((n // s, W, s), BF16),
            jax.ShapeDtypeStruct((n // s, W, s), BF16),
            jax.ShapeDtypeStruct((n // s, s // tq, N_HEADS * V_ROWS, tq), BF16),
            jax.ShapeDtypeStruct((n // s, s // kb_sb, W, kb_sb), BF16),
            jax.ShapeDtypeStruct((n // s, N_IDX_HEADS, s), F32),
        ],
        compiler_params=pltpu.CompilerParams(
            dimension_semantics=("arbitrary",), vmem_limit_bytes=VMEM_LIMIT_BYTES),
        name="pre_mix_proj",
    )(x2, g, w_packed)


def _sparse_kernel(qiT_ref, kw_ref, wiT_ref, qT_ref, k_ref, vT_ref, bias_ref, o_ref,
                   key_sc, hi_sc, lo_sc, lo2_sc, msk_sc, lg_sc, mb_sc, oT_sc, on_sc, m_sc, *, tq, topk):
    kb = tq
    i = pl.program_id(1)
    t0 = i * tq
    nkb = i + 1
    row = lax.broadcasted_iota(jnp.int32, (kb, tq), 0)
    col = lax.broadcasted_iota(jnp.int32, (kb, tq), 1)
    w = wiT_ref[0] * (N_IDX_HEADS ** -0.5)

    def blk_slice(j):
        return pl.ds(pl.multiple_of(j * kb, kb), kb)

    def score_block(j):
        ks = blk_slice(j)
        kib = kw_ref[0, ks, 0:IDX_DIM].astype(BF16)
        acc = jnp.zeros((kb, tq), F32)
        for h in range(N_IDX_HEADS):
            d = jnp.dot(kib, qiT_ref[0, h * IDX_DIM:(h + 1) * IDX_DIM, :],
                        preferred_element_type=F32)
            acc = acc + jnp.maximum(d, 0.0) * w[h:h + 1, :]
        adm = ((j * kb + row) // CHUNK) <= ((t0 + col) // CHUNK)
        bits = pltpu.bitcast(acc, jnp.int32)
        key = bits ^ ((bits >> 31) & 0x7FFFFFFF)
        key = jnp.where(adm, key, INT_MIN)
        key_sc[ks, :] = key
        hi_sc[ks, :] = (key >> 16).astype(jnp.int16)
        lo_sc[ks, :] = ((key & 0xFFFF) - 32768).astype(jnp.int16)

    def score_pair(jj, _):
        score_block(2 * jj)
        score_block(jnp.minimum(2 * jj + 1, nkb - 1))
        return 0

    lax.fori_loop(0, (nkb + 1) // 2, score_pair, 0)

    def count16(ref, pred):
        one = jnp.ones((), BF16)
        zero = jnp.zeros((), BF16)

        def body(j, accs):
            hit = jnp.where(pred(ref[blk_slice(j), :]), one, zero).reshape(kb // 16, 16, tq)
            accs = list(accs)
            for r in range(kb // 16):
                accs[r % 4] = accs[r % 4] + hit[r]
            return tuple(accs)

        accs = lax.fori_loop(0, nkb, body, (jnp.zeros((16, tq), BF16),) * 4)
        tot = (accs[0].astype(F32) + accs[1].astype(F32)) + (accs[2].astype(F32) + accs[3].astype(F32))
        return jnp.sum(tot, axis=0, keepdims=True)

    def search16(ref, kth, checkpoints=()):
        def bit_step(b, st):
            thr, taken_at_thr = st
            cand = thr + lax.shift_left(jnp.int32(1), 15 - b)
            cand16 = cand.astype(jnp.int16)
            c = count16(ref, lambda blk: blk >= cand16)
            ok = c >= kth
            return jnp.where(ok, cand, thr), jnp.where(ok, c, taken_at_thr)

        def finish(st, first, rest):
            if not rest:
                return lax.fori_loop(first, 16, bit_step, st)
            st = lax.fori_loop(first, rest[0], bit_step, st)
            unresolved = jnp.max(jnp.where(st[1] != kth, 1.0, 0.0)) > 0.0
            return lax.cond(unresolved, lambda s: finish(s, rest[0], rest[1:]), lambda s: s, st)

        st = (jnp.full((1, tq), -32768, jnp.int32), jnp.full((1, tq), -1.0, F32))
        return finish(st, 0, tuple(checkpoints))[0]

    thr_hi = search16(hi_sc, float(topk))
    thr_hi16 = thr_hi.astype(jnp.int16)

    def low_half_block(j, accs):
        ks = blk_slice(j)
        hi = hi_sc[ks, :]
        lo2_sc[ks, :] = jnp.where(hi == thr_hi16, lo_sc[ks, :], jnp.int16(-32768))
        hit = jnp.where(hi > thr_hi16, jnp.ones((), BF16), jnp.zeros((), BF16)).reshape(kb // 16, 16, tq)
        accs = list(accs)
        for r in range(kb // 16):
            accs[r % 4] = accs[r % 4] + hit[r]
        return tuple(accs)

    accs = lax.fori_loop(0, nkb, low_half_block, (jnp.zeros((16, tq), BF16),) * 4)
    above_hi = jnp.sum((accs[0].astype(F32) + accs[1].astype(F32))
                       + (accs[2].astype(F32) + accs[3].astype(F32)), axis=0, keepdims=True)
    thr_lo = search16(lo2_sc, topk - above_hi, LOW_HALF_CHECKS)
    thr_lo16 = thr_lo.astype(jnp.int16)
    thr = thr_hi * 65536 + (thr_lo + 32768)

    floor = jnp.maximum(thr, INT_MIN + 1)

    def mask_block(j, cnt):
        ks = blk_slice(j)
        sel = key_sc[ks, :] >= floor
        msk_sc[ks, :] = jnp.where(sel, 0.0, MASK_NEG)
        return cnt + jnp.sum(jnp.where(sel, 1.0, 0.0).reshape(kb // 8, 8, tq), axis=0)

    taken = jnp.sum(lax.fori_loop(0, nkb, mask_block, jnp.zeros((8, tq), F32)), axis=0, keepdims=True)
    admissible = ((t0 + col[0:1, :]) // CHUNK + 1) * CHUNK
    spare = jnp.max(jnp.where(taken != jnp.minimum(topk, admissible).astype(F32), 1.0, 0.0)) > 0.0

    @pl.when(spare)
    def _():
        need = topk - (above_hi + count16(lo2_sc, lambda blk: blk > thr_lo16))
        lower = (lax.broadcasted_iota(jnp.int32, (kb, kb), 1)
                 < lax.broadcasted_iota(jnp.int32, (kb, kb), 0)).astype(BF16)

        def ranked_block(j, tie_carry):
            ks = blk_slice(j)
            blk = key_sc[ks, :]
            tie = blk == thr
            tie_f = jnp.where(tie, 1.0, 0.0)
            rank = jnp.dot(lower, tie_f.astype(BF16), preferred_element_type=F32) + tie_carry
            sel = ((blk > thr) | (tie & (rank < need))) & (blk != INT_MIN)
            msk_sc[ks, :] = jnp.where(sel, 0.0, MASK_NEG)
            return tie_carry + jnp.sum(tie_f, axis=0, keepdims=True)

        lax.fori_loop(0, nkb, ranked_block, jnp.zeros((1, tq), F32))

    oT_sc[...] = jnp.zeros_like(oT_sc)
    m_sc[...] = jnp.full((N_HEADS, tq), M_INIT, F32)

    def logits_stage(j, slot):
        ks = blk_slice(j)
        d = jnp.minimum(i - j, 2)
        msk = msk_sc[ks, :]
        for h in range(N_HEADS):
            hs = slice(h * HEAD_DIM, (h + 1) * HEAD_DIM)
            lg = jnp.dot(k_ref[0, ks, hs], qT_ref[0, hs, :],
                         preferred_element_type=F32)
            lg = lg + bias_ref[h, d] + msk
            lg_sc[slot, h] = lg
            mb_sc[slot, h:h + 1, :] = jnp.max(lg, axis=0, keepdims=True)

    def softmax_stage(j, slot):
        m_old = m_sc[...]
        m_new = jnp.maximum(m_old, mb_sc[slot])
        alpha = jnp.exp(m_old - m_new)
        for h in range(N_HEADS):
            vs = slice(h * V_ROWS, (h + 1) * V_ROWS)
            p = jnp.exp(lg_sc[slot, h] - m_new[h:h + 1, :])
            oT_sc[vs, :] = alpha[h:h + 1, :] * oT_sc[vs, :] + jnp.dot(
                vT_ref[0, j, vs, :], p.astype(BF16), preferred_element_type=F32)
        m_sc[...] = m_new

    logits_stage(0, 0)

    def attn_block(j, _):
        slot = j & 1
        softmax_stage(j - 1, 1 - slot)
        logits_stage(j, slot)
        return 0

    lax.fori_loop(1, nkb, attn_block, 0)
    softmax_stage(nkb - 1, (nkb - 1) & 1)
    for h in range(N_HEADS):
        acc = oT_sc[h * V_ROWS:h * V_ROWS + HEAD_DIM, :]
        l = oT_sc[h * V_ROWS + HEAD_DIM:h * V_ROWS + HEAD_DIM + 1, :]
        on_sc[h * HEAD_DIM:(h + 1) * HEAD_DIM, :] = acc / l
    o_ref[0] = on_sc[...].T.astype(BF16)


def _sparse(qiT, kw, wiT, qT, k, vT, bias_tiles, tq, topk):
    b, s, _ = k.shape
    W = W_ATT
    kern = functools.partial(_sparse_kernel, tq=tq, topk=topk)
    return pl.pallas_call(
        kern,
        grid=(b, s // tq),
        in_specs=[
            pl.BlockSpec((1, W, tq), lambda bi, i: (bi, 0, i)),
            pl.BlockSpec((1, s, 128), lambda bi, i: (bi, 0, 0)),
            pl.BlockSpec((1, N_IDX_HEADS, tq), lambda bi, i: (bi, 0, i)),
            pl.BlockSpec((1, W, tq), lambda bi, i: (bi, 0, i)),
            pl.BlockSpec((1, s, W), lambda bi, i: (bi, 0, 0)),
            pl.BlockSpec((1, s // tq, N_HEADS * V_ROWS, tq), lambda bi, i: (bi, 0, 0, 0)),
            pl.BlockSpec((N_HEADS, 3, tq, tq), lambda bi, i: (0, 0, 0, 0),
                         pipeline_mode=pl.Buffered(1)),
        ],
        out_specs=pl.BlockSpec((1, tq, W), lambda bi, i: (bi, i, 0)),
        out_shape=jax.ShapeDtypeStruct((b, s, W), BF16),
        scratch_shapes=[
            pltpu.VMEM((s, tq), jnp.int32),
            pltpu.VMEM((s, tq), jnp.int16),
            pltpu.VMEM((s, tq), jnp.int16),
            pltpu.VMEM((s, tq), jnp.int16),
            pltpu.VMEM((s, tq), F32),
            pltpu.VMEM((2, N_HEADS, tq, tq), F32),
            pltpu.VMEM((2, N_HEADS, tq), F32),
            pltpu.VMEM((N_HEADS * V_ROWS, tq), F32),
            pltpu.VMEM((W, tq), F32),
            pltpu.VMEM((N_HEADS, tq), F32),
        ],
        compiler_params=pltpu.CompilerParams(
            dimension_semantics=("arbitrary", "arbitrary"), vmem_limit_bytes=VMEM_LIMIT_BYTES),
        name="sparse_attention",
    )(qiT, kw, wiT, qT, k, vT, bias_tiles)


def _sb_kernel(qT_ref, k_ref, vT_ref, o_ref, z_sc, hi_sc, lo_sc, a_sc, accT_sc, *, tq, kb):
    i = pl.program_id(1)
    t0 = i * tq
    nb = tq // kb
    row = lax.broadcasted_iota(jnp.int32, (kb, tq), 0)
    col = lax.broadcasted_iota(jnp.int32, (kb, tq), 1)
    upper = (lax.broadcasted_iota(jnp.int32, (kb, kb), 1)
             >= lax.broadcasted_iota(jnp.int32, (kb, kb), 0)).astype(BF16)
    accT_sc[...] = jnp.zeros_like(accT_sc)

    def causal(j):
        return (j * kb + row) < (t0 + col)

    def keep_stage(j, slot, valid, l0=0):
        ks = pl.ds(pl.multiple_of(j * kb, kb), kb)
        sums = []
        for h in range(N_HEADS):
            hs = slice(h * HEAD_DIM, (h + 1) * HEAD_DIM)
            z = jnp.dot(k_ref[0, ks, hs], qT_ref[0, hs, l0:],
                        preferred_element_type=F32)
            lk = -(jnp.maximum(z, 0.0) + jnp.log(1.0 + jnp.exp(-jnp.abs(z))))
            if valid is not None:
                lk = jnp.where(valid[:, l0:], lk, 0.0)
            hi = lk.astype(BF16)
            z_sc[slot, h, :, l0:] = z
            hi_sc[slot, h, :, l0:] = hi
            lo_sc[slot, h, :, l0:] = (lk - hi.astype(F32)).astype(BF16)
            sums.append(jnp.sum(lk, axis=0, keepdims=True))
        sums = jnp.concatenate(sums, axis=0)
        if l0:
            sums = jnp.concatenate([jnp.zeros((N_HEADS, l0), F32), sums], axis=1)
        return sums

    def weight_stage(slot, carry, valid, l0=0):
        for h in range(N_HEADS):
            suf = (jnp.dot(upper, hi_sc[slot, h, :, l0:], preferred_element_type=F32)
                   + jnp.dot(upper, lo_sc[slot, h, :, l0:], preferred_element_type=F32))
            a = jnp.exp(z_sc[slot, h, :, l0:] + suf + carry[h:h + 1, l0:])
            if valid is not None:
                a = jnp.where(valid[:, l0:], a, 0.0)
            a_sc[h, :, l0:] = a.astype(BF16)

    def value_stage(j, when, l0=0):
        @pl.when(when)
        def _():
            for h in range(N_HEADS):
                hs = slice(h * HEAD_DIM, (h + 1) * HEAD_DIM)
                accT_sc[hs, l0:] += jnp.dot(vT_ref[0, j, hs, :], a_sc[h, :, l0:],
                                            preferred_element_type=F32)

    j_top = (i + 1) * nb - 1
    first_lane = [(nb - 1 - r) * kb for r in range(nb)]
    carries = [jnp.zeros((N_HEADS, tq), F32)]
    carries.append(keep_stage(j_top, 0, causal(j_top), first_lane[0]))
    for r in range(1, nb):
        weight_stage((r - 1) % 2, carries[r - 1], causal(j_top - r + 1), first_lane[r - 1])
        carries.append(carries[r] + keep_stage(j_top - r, r % 2, causal(j_top - r), first_lane[r]))
        value_stage(j_top - r + 1, i >= 0, first_lane[r - 1])
    last_slot = (nb - 1) % 2
    j_last = j_top - nb + 1
    c_pend, c_next = carries[nb - 1], carries[nb]

    def only_diagonal():
        weight_stage(last_slot, c_pend, causal(j_last))
        value_stage(j_last, i == 0)

    def more_blocks():
        weight_stage(last_slot, c_pend, causal(j_last))
        c2 = c_next + keep_stage(j_last - 1, 1 - last_slot, None)
        value_stage(j_last, i >= 1)

        def cond(st):
            j, _, _, c_nxt = st
            return jnp.logical_and(j >= 0, jnp.max(c_nxt) > SB_EXIT)

        def body(st):
            j, slot, c_pnd, c_nxt = st
            weight_stage(slot, c_pnd, None)
            sums = keep_stage(j, 1 - slot, None)
            value_stage(j + 1, j >= 0)
            return j - 1, 1 - slot, c_nxt, c_nxt + sums

        j, slot, c_pnd, _ = lax.while_loop(
            cond, body, (j_last - 2, jnp.int32(1 - last_slot), c_next, c2))
        weight_stage(slot, c_pnd, None)
        value_stage(j + 1, i >= 1)

    lax.cond(i >= 1, more_blocks, only_diagonal)
    o_ref[0] = accT_sc[...].T.astype(BF16)


def _stick_breaking(qT, k, vT, tq, kb):
    b, s, _ = k.shape
    W = W_ATT
    kern = functools.partial(_sb_kernel, tq=tq, kb=kb)
    return pl.pallas_call(
        kern,
        grid=(b, s // tq),
        in_specs=[
            pl.BlockSpec((1, W, tq), lambda bi, i: (bi, 0, i)),
            pl.BlockSpec((1, s, W), lambda bi, i: (bi, 0, 0)),
            pl.BlockSpec((1, s // kb, W, kb), lambda bi, i: (bi, 0, 0, 0)),
        ],
        out_specs=pl.BlockSpec((1, tq, W), lambda bi, i: (bi, i, 0)),
        out_shape=jax.ShapeDtypeStruct((b, s, W), BF16),
        scratch_shapes=[
            pltpu.VMEM((2, N_HEADS, kb, tq), F32),
            pltpu.VMEM((2, N_HEADS, kb, tq), BF16),
            pltpu.VMEM((2, N_HEADS, kb, tq), BF16),
            pltpu.VMEM((N_HEADS, kb, tq), BF16),
            pltpu.VMEM((W, tq), F32),
        ],
        compiler_params=pltpu.CompilerParams(
            dimension_semantics=("arbitrary", "arbitrary"), vmem_limit_bytes=VMEM_LIMIT_BYTES),
        name="stick_breaking_attention",
    )(qT, k, vT)


def _merge_ffn_kernel(x_ref, oa_ref, ob_ref, gates_ref, wa_ref, wb_ref, wo_ref, gmix_ref,
                      gpre_ref, wgu_ref, wd_ref, gpost_ref, xo_ref, x1_sc, acc_sc, *, dff, fc):
    d = x_ref.shape[1]
    pa = jnp.dot(oa_ref[...], wa_ref[...], preferred_element_type=F32)
    pb = jnp.dot(ob_ref[...], wb_ref[...], preferred_element_type=F32)
    m = jax.nn.sigmoid(gates_ref[:, 0:d]) * pa + jax.nn.sigmoid(gates_ref[:, d:2 * d]) * pb
    y = jnp.dot(m.astype(BF16), wo_ref[...], preferred_element_type=F32)
    x1_sc[...] = x_ref[...] + _rms(y, gmix_ref[...])
    h = _rms(x1_sc[...], gpre_ref[...]).astype(BF16)
    for c in range(0, dff, fc):
        gate = jnp.dot(h, wgu_ref[:, c:c + fc], preferred_element_type=F32)
        up = jnp.dot(h, wgu_ref[:, dff + c:dff + c + fc], preferred_element_type=F32)
        act = (jax.nn.silu(gate) * up).astype(BF16)
        part = jnp.dot(act, wd_ref[c:c + fc, :], preferred_element_type=F32)
        if c == 0:
            acc_sc[...] = part
        else:
            acc_sc[...] += part
    xo_ref[...] = x1_sc[...] + _rms(acc_sc[...], gpost_ref[...])


def _merge_ffn(x2, oa, ob, gates, wa, wb, wo, gmix, gpre, wgu, wd, gpost, tm, fc):
    n, d = x2.shape
    W = W_ATT
    dff = wd.shape[0]
    row = lambda i: (i, 0)
    const = lambda i: (0, 0)
    resident = functools.partial(pl.BlockSpec, index_map=const, pipeline_mode=pl.Buffered(1))
    kern = functools.partial(_merge_ffn_kernel, dff=dff, fc=fc)
    return pl.pallas_call(
        kern,
        grid=(n // tm,),
        in_specs=[
            pl.BlockSpec((tm, d), row),
            pl.BlockSpec((tm, W), row),
            pl.BlockSpec((tm, W), row),
            pl.BlockSpec((tm, 2 * d), row),
            resident((W, d)),
            resident((W, d)),
            resident((d, d)),
            pl.BlockSpec((1, d), const),
            pl.BlockSpec((1, d), const),
            resident((d, 2 * dff)),
            resident((dff, d)),
            pl.BlockSpec((1, d), const),
        ],
        out_specs=pl.BlockSpec((tm, d), row),
        out_shape=jax.ShapeDtypeStruct((n, d), F32),
        scratch_shapes=[pltpu.VMEM((tm, d), F32), pltpu.VMEM((tm, d), F32)],
        compiler_params=pltpu.CompilerParams(
            dimension_semantics=("arbitrary",), vmem_limit_bytes=VMEM_LIMIT_BYTES),
        name="merge_swiglu",
    )(x2, oa, ob, gates, wa, wb, wo, gmix, gpre, wgu, wd, gpost)


def _t5_bucket(rel):
    nb = N_BUCKETS // 2
    max_exact = nb // 2
    ret = (rel > 0).astype(jnp.int32) * nb
    n = jnp.abs(rel)
    nf = jnp.maximum(n, 1).astype(jnp.float32)
    large = max_exact + (jnp.log(nf / max_exact) / math.log(MAX_DISTANCE / max_exact)
                         * (nb - max_exact)).astype(jnp.int32)
    large = jnp.minimum(large, nb - 1)
    return ret + jnp.where(n < max_exact, n, large)


def _bias_tables(rel_bias, tq):
    s_l = jnp.arange(tq, dtype=jnp.int32)[:, None]
    t_l = jnp.arange(tq, dtype=jnp.int32)[None, :]
    rel = jnp.stack([s_l - t_l - d * tq for d in range(3)])
    onehot = (_t5_bucket(rel)[..., None] == jnp.arange(N_BUCKETS)).astype(F32)
    return jnp.einsum("dstb,bh->hdst", onehot, rel_bias.astype(F32), precision=lax.Precision.HIGHEST)


def _pack_w_in(w):
    d = w.shape[0]
    head = 3 * W_ATT + N_IDX_HEADS * IDX_DIM
    small = IDX_DIM + N_IDX_HEADS
    pad = jnp.zeros((d, 128 - small), w.dtype)
    return jnp.concatenate([w[:, :head], w[:, head + small:], w[:, head:head + small], pad],
                           axis=1).astype(BF16)


def kernel(x, w_in, w_branch_sparse, w_branch_sb, w_out, w_gate_up, w_down,
           g_pre_mix, g_post_mix, g_pre_ffn, g_post_ffn, rel_bias):
    b, s, d = x.shape
    depth = w_in.shape[0]
    n = b * s
    W = W_ATT
    tm = min(512, s)
    tq = min(256, s)
    kb_sb = min(128, tq)
    topk = min(INDEX_TOPK, s // 4)
    dff = w_down.shape[1]
    fc = 256
    assert s % tm == 0 and tm % tq == 0 and dff % fc == 0 and tq % CHUNK == 0

    bias_tiles = _bias_tables(rel_bias, tq)
    x2 = x.reshape(n, d)
    for l in range(depth):
        ka, kb, gates, kw, qaT, qiT, qbT, vTa, vTb, wiT = _proj(
            x2, g_pre_mix[l][None], _pack_w_in(w_in[l]), tm, tq, kb_sb, s)
        o_a = _sparse(qiT, kw.reshape(b, s, 128), wiT, qaT, ka.reshape(b, s, W), vTa,
                      bias_tiles, tq, topk)
        o_b = _stick_breaking(qbT, kb.reshape(b, s, W), vTb, tq, kb_sb)
        x2 = _merge_ffn(x2, o_a.reshape(n, W), o_b.reshape(n, W), gates,
                        w_branch_sparse[l].astype(BF16), w_branch_sb[l].astype(BF16),
                        w_out[l].astype(BF16), g_post_mix[l][None], g_pre_ffn[l][None],
                        w_gate_up[l].astype(BF16), w_down[l].astype(BF16), g_post_ffn[l][None], tm, fc)
    return x2.reshape(b, s, d)
```
